```python
import jax, jax.numpy as jnp
from jax import lax
import numpy as np

D_MODEL = 2048
BATCH = 2
SEQ = 4096
DEPTH = 4
DEC_BATCH = 8
DEC_SEQ = 8
PAST_LEN = 16384
PAGE_SIZE = 128

N_HEADS = 16
HEAD_DIM = D_MODEL // N_HEADS
D_FF = 5632
CONV_W = 3
ROPE_THETA = 10000.0
MOBA_BLOCK = 256
MOBA_TOPK = 3
Q_BLOCK = 128
MOBA_Q_CHUNK = 16
FORGET_BIAS_INIT = 4.0
EPS = 1e-6
N_MIXERS = 2
N_FOX = (DEPTH + 1) // 2
N_MOBA = DEPTH // 2
ATTN_SCALE = HEAD_DIM ** -0.5

kernel_name = 'fox_moba_convffn_hybrid_step'


def rms_norm(x, g):
    xf = x.astype(jnp.float32)
    y = xf * lax.rsqrt(jnp.mean(xf * xf, axis=-1, keepdims=True) + EPS)
    return (y * g.astype(jnp.float32)).astype(x.dtype)


def rope(x, pos):
    half = HEAD_DIM // 2
    inv_freq = ROPE_THETA ** (-jnp.arange(half, dtype=jnp.float32) / half)
    ang = pos.astype(jnp.float32)[:, None] * inv_freq[None, :]
    cos = jnp.cos(ang)[None, :, None, :]
    sin = jnp.sin(ang)[None, :, None, :]
    xf = x.astype(jnp.float32)
    x1, x2 = xf[..., :half], xf[..., half:]
    return jnp.concatenate([x1 * cos - x2 * sin, x2 * cos + x1 * sin], axis=-1).astype(x.dtype)


def gather_past(cache, j, page_table):
    rows = cache[j, page_table]
    return rows.reshape((rows.shape[0], rows.shape[1] * rows.shape[2]) + rows.shape[3:])


def fox_project(h, w_in, b_f, q_gain, k_gain):
    B, T = h.shape[:2]
    proj = h @ w_in
    q = proj[..., :D_MODEL].reshape(B, T, N_HEADS, HEAD_DIM)
    k = proj[..., D_MODEL:2 * D_MODEL].reshape(B, T, N_HEADS, HEAD_DIM)
    v = proj[..., 2 * D_MODEL:3 * D_MODEL].reshape(B, T, N_HEADS, HEAD_DIM)
    f_logit = proj[..., 3 * D_MODEL:] + b_f
    logf = jax.nn.log_sigmoid(f_logit.astype(jnp.float32))
    return rms_norm(q, q_gain), rms_norm(k, k_gain), v, logf


def fox_attend(q, cq, q_pos, k, v, ck, k_pos):
    s = jnp.einsum('bqhd,bkhd->bhqk', q, k, preferred_element_type=jnp.float32) * ATTN_SCALE
    bias = jnp.swapaxes(cq, 1, 2)[:, :, :, None] - jnp.swapaxes(ck, 1, 2)[:, :, None, :]
    causal = k_pos[None, :] <= q_pos[:, None]
    p = jax.nn.softmax(jnp.where(causal, s + bias, -jnp.inf), axis=-1)
    return jnp.einsum('bhqk,bkhd->bqhd', p.astype(v.dtype), v)


def fox_prompt(q, k, v, logf):
    B, T = q.shape[:2]
    c = lax.cumsum(logf.astype(jnp.float32), axis=1)
    pos = jnp.arange(T, dtype=jnp.int32)
    nb = T // Q_BLOCK
    qb = jnp.moveaxis(q.reshape(B, nb, Q_BLOCK, N_HEADS, HEAD_DIM), 1, 0)
    cb = jnp.moveaxis(c.reshape(B, nb, Q_BLOCK, N_HEADS), 1, 0)
    pb = pos.reshape(nb, Q_BLOCK)
    ob = lax.map(lambda a: fox_attend(a[0], a[1], a[2], k, v, c, pos), (qb, cb, pb))
    return jnp.moveaxis(ob, 0, 1).reshape(B, T, N_HEADS, HEAD_DIM)


def fox_sample(q, k_new, v_new, logf_new, past_k, past_v, past_logf):
    P, T = past_k.shape[1], q.shape[1]
    k = jnp.concatenate([past_k, k_new.astype(past_k.dtype)], axis=1)
    v = jnp.concatenate([past_v, v_new.astype(past_v.dtype)], axis=1)
    lf = jnp.concatenate([past_logf.astype(jnp.float32), logf_new.astype(jnp.float32)], axis=1)
    c = lax.cumsum(lf, axis=1)
    k_pos = jnp.arange(P + T, dtype=jnp.int32)
    q_pos = P + jnp.arange(T, dtype=jnp.int32)
    return fox_attend(q, c[:, P:], q_pos, k, v, c, k_pos)


def moba_project(h, pos, w_in, q_gain, k_gain):
    B, T = h.shape[:2]
    proj = h @ w_in
    q = proj[..., :D_MODEL].reshape(B, T, N_HEADS, HEAD_DIM)
    k = proj[..., D_MODEL:2 * D_MODEL].reshape(B, T, N_HEADS, HEAD_DIM)
    v = proj[..., 2 * D_MODEL:].reshape(B, T, N_HEADS, HEAD_DIM)
    return rope(rms_norm(q, q_gain), pos), rope(rms_norm(k, k_gain), pos), v


def moba_blocks(k, v):
    B, L = k.shape[:2]
    nb = -(-L // MOBA_BLOCK)
    pad = ((0, 0), (0, nb * MOBA_BLOCK - L), (0, 0), (0, 0))
    kb = jnp.pad(k, pad).reshape(B, nb, MOBA_BLOCK, N_HEADS, HEAD_DIM).transpose(0, 3, 1, 2, 4)
    vb = jnp.pad(v, pad).reshape(B, nb, MOBA_BLOCK, N_HEADS, HEAD_DIM).transpose(0, 3, 1, 2, 4)
    k_mean = jnp.mean(kb.astype(jnp.float32), axis=3)
    return kb, vb, k_mean


def moba_attend(q, q_pos, kb, vb, k_mean):
    B, Tq = q.shape[:2]
    nb = kb.shape[2]
    n_sel = min(MOBA_TOPK, nb)
    own = (q_pos // MOBA_BLOCK).astype(jnp.int32)
    gate = jnp.einsum('bqhd,bhnd->bhqn', q.astype(jnp.float32), k_mean)
    fully_past = jnp.arange(nb, dtype=jnp.int32)[None, :] < own[:, None]
    gate = jnp.where(fully_past, gate, -jnp.inf)
    g_val, sel = lax.top_k(gate, n_sel)
    own_idx = jnp.broadcast_to(own[None, None, :, None], (B, N_HEADS, Tq, 1))
    idx = jnp.concatenate([sel.astype(jnp.int32), own_idx], axis=-1)
    slot_ok = jnp.concatenate([g_val > -jnp.inf, jnp.ones((B, N_HEADS, Tq, 1), dtype=jnp.bool_)], axis=-1)
    b_ar = jnp.arange(B)[:, None, None, None]
    h_ar = jnp.arange(N_HEADS)[None, :, None, None]
    kg = kb[b_ar, h_ar, idx]
    vg = vb[b_ar, h_ar, idx]
    s = jnp.einsum('bqhd,bhqnpd->bhqnp', q, kg, preferred_element_type=jnp.float32) * ATTN_SCALE
    key_pos = idx[..., None] * MOBA_BLOCK + jnp.arange(MOBA_BLOCK, dtype=jnp.int32)
    valid = slot_ok[..., None] & (key_pos <= q_pos[None, None, :, None, None])
    s = jnp.where(valid, s, -jnp.inf)
    p = jax.nn.softmax(s.reshape(B, N_HEADS, Tq, -1), axis=-1).reshape(s.shape)
    return jnp.einsum('bhqnp,bhqnpd->bqhd', p.astype(vg.dtype), vg)


def moba_prompt(q, k, v):
    B, T = q.shape[:2]
    kb, vb, k_mean = moba_blocks(k, v)
    nc = T // MOBA_Q_CHUNK
    qc = jnp.moveaxis(q.reshape(B, nc, MOBA_Q_CHUNK, N_HEADS, HEAD_DIM), 1, 0)
    pc = jnp.arange(T, dtype=jnp.int32).reshape(nc, MOBA_Q_CHUNK)
    oc = lax.map(lambda a: moba_attend(a[0], a[1], kb, vb, k_mean), (qc, pc))
    return jnp.moveaxis(oc, 0, 1).reshape(B, T, N_HEADS, HEAD_DIM)


def moba_sample(q, k_new, v_new, past_k, past_v, q_pos):
    k = jnp.concatenate([past_k, k_new.astype(past_k.dtype)], axis=1)
    v = jnp.concatenate([past_v, v_new.astype(past_v.dtype)], axis=1)
    kb, vb, k_mean = moba_blocks(k, v)
    return moba_attend(q, q_pos, kb, vb, k_mean)


def conv_ffn(h, conv_state, w_gate, w_up, conv_w, conv_b, w_down):
    T = h.shape[1]
    g = h @ w_gate
    u = h @ w_up
    gx = jnp.concatenate([conv_state.astype(g.dtype), g], axis=1)
    acc = conv_b + gx[:, 0:T] * conv_w[0]
    for j in range(1, CONV_W):
        acc = acc + gx[:, j:j + T] * conv_w[j]
    y = (jax.nn.silu(acc) * u) @ w_down
    return y, gx[:, -(CONV_W - 1):]


def setup_inputs(seed: int = 0) -> dict:
    key = jax.random.key(seed)
    ks = jax.random.split(key, 32)
    n_pages = PAST_LEN // PAGE_SIZE
    n_used = DEC_BATCH * n_pages
    n_pool = n_used + (n_used + 3) // 4
    page_table = jax.random.permutation(ks[0], n_pool)[:n_used].reshape(DEC_BATCH, n_pages).astype(jnp.int32)

    def nrm(k, shape, scale=1.0):
        return jax.random.normal(k, shape, jnp.float32) * scale

    kv_fox = (N_FOX, n_pool, PAGE_SIZE, N_HEADS, HEAD_DIM)
    kv_moba = (N_MOBA, n_pool, PAGE_SIZE, N_HEADS, HEAD_DIM)
    sd = D_MODEL ** -0.5
    return {
        'x_prompt': nrm(ks[1], (BATCH, SEQ, D_MODEL)),
        'x_sample': nrm(ks[2], (DEC_BATCH, DEC_SEQ, D_MODEL)),
        'cache_fox_k': nrm(ks[3], kv_fox),
        'cache_fox_v': nrm(ks[4], kv_fox),
        'cache_fox_logf': jax.nn.log_sigmoid(FORGET_BIAS_INIT + nrm(ks[5], (N_FOX, n_pool, PAGE_SIZE, N_HEADS))),
        'cache_moba_k': nrm(ks[6], kv_moba),
        'cache_moba_v': nrm(ks[7], kv_moba),
        'state_ffn_conv': nrm(ks[8], (DEPTH, DEC_BATCH, CONV_W - 1, D_FF)),
        'page_table': page_table,
        'attn_norm': 1.0 + nrm(ks[9], (DEPTH, D_MODEL), 0.01),
        'ffn_norm': 1.0 + nrm(ks[10], (DEPTH, D_MODEL), 0.01),
        'fox_w_in': nrm(ks[11], (N_FOX, D_MODEL, 3 * D_MODEL + N_HEADS), sd),
        'fox_b_f': FORGET_BIAS_INIT + nrm(ks[12], (N_FOX, N_HEADS), 0.1),
        'fox_q_norm': 1.0 + nrm(ks[13], (N_FOX, HEAD_DIM), 0.01),
        'fox_k_norm': 1.0 + nrm(ks[14], (N_FOX, HEAD_DIM), 0.01),
        'fox_w_o': nrm(ks[15], (N_FOX, D_MODEL, D_MODEL), sd),
        'moba_w_in': nrm(ks[16], (N_MOBA, D_MODEL, 3 * D_MODEL), sd),
        'moba_q_norm': 1.0 + nrm(ks[17], (N_MOBA, HEAD_DIM), 0.01),
        'moba_k_norm': 1.0 + nrm(ks[18], (N_MOBA, HEAD_DIM), 0.01),
        'moba_w_o': nrm(ks[19], (N_MOBA, D_MODEL, D_MODEL), sd),
        'ffn_w_gate': nrm(ks[20], (DEPTH, D_MODEL, D_FF), sd),
        'ffn_w_up': nrm(ks[21], (DEPTH, D_MODEL, D_FF), sd),
        'ffn_conv_w': nrm(ks[22], (DEPTH, CONV_W, D_FF), CONV_W ** -0.5),
        'ffn_conv_b': nrm(ks[23], (DEPTH, D_FF), 0.01),
        'ffn_w_down': nrm(ks[24], (DEPTH, D_FF, D_MODEL), D_FF ** -0.5),
    }


def reference(x_prompt, x_sample, cache_fox_k, cache_fox_v, cache_fox_logf,
              cache_moba_k, cache_moba_v, state_ffn_conv, page_table,
              attn_norm, ffn_norm, fox_w_in, fox_b_f, fox_q_norm, fox_k_norm, fox_w_o,
              moba_w_in, moba_q_norm, moba_k_norm, moba_w_o,
              ffn_w_gate, ffn_w_up, ffn_conv_w, ffn_conv_b, ffn_w_down):
    B, T = x_prompt.shape[:2]
    Bs, Ts = x_sample.shape[:2]
    past_len = page_table.shape[1] * PAGE_SIZE
    pos_p = jnp.arange(T, dtype=jnp.int32)
    pos_s = past_len + jnp.arange(Ts, dtype=jnp.int32)
    xp, xs = x_prompt, x_sample
    fkp, fvp, flp, fks, fvs, fls = [], [], [], [], [], []
    mkp, mvp, mks, mvs = [], [], [], []
    cvp, cvs = [], []
    for i in range(DEPTH):
        j = i // N_MIXERS
        hp = rms_norm(xp, attn_norm[i])
        hs = rms_norm(xs, attn_norm[i])
        if i % N_MIXERS == 0:
            qp, kp, vp, lp = fox_project(hp, fox_w_in[j], fox_b_f[j], fox_q_norm[j], fox_k_norm[j])
            qs, ks, vs, ls = fox_project(hs, fox_w_in[j], fox_b_f[j], fox_q_norm[j], fox_k_norm[j])
            op = fox_prompt(qp, kp, vp, lp)
            os_ = fox_sample(qs, ks, vs, ls,
                             gather_past(cache_fox_k, j, page_table),
                             gather_past(cache_fox_v, j, page_table),
                             gather_past(cache_fox_logf, j, page_table))
            w_o = fox_w_o[j]
            fkp.append(kp); fvp.append(vp); flp.append(lp)
            fks.append(ks); fvs.append(vs); fls.append(ls)
        else:
            qp, kp, vp = moba_project(hp, pos_p, moba_w_in[j], moba_q_norm[j], moba_k_norm[j])
            qs, ks, vs = moba_project(hs, pos_s, moba_w_in[j], moba_q_norm[j], moba_k_norm[j])
            op = moba_prompt(qp, kp, vp)
            os_ = moba_sample(qs, ks, vs,
                              gather_past(cache_moba_k, j, page_table),
                              gather_past(cache_moba_v, j, page_table), pos_s)
            w_o = moba_w_o[j]
            mkp.append(kp); mvp.append(vp)
            mks.append(ks); mvs.append(vs)
        xp = xp + op.reshape(B, T, D_MODEL) @ w_o
        xs = xs + os_.reshape(Bs, Ts, D_MODEL) @ w_o
        yp, sp = conv_ffn(rms_norm(xp, ffn_norm[i]), jnp.zeros((B, CONV_W - 1, D_FF), xp.dtype),
                          ffn_w_gate[i], ffn_w_up[i], ffn_conv_w[i], ffn_conv_b[i], ffn_w_down[i])
        ys, ss = conv_ffn(rms_norm(xs, ffn_norm[i]), state_ffn_conv[i],
                          ffn_w_gate[i], ffn_w_up[i], ffn_conv_w[i], ffn_conv_b[i], ffn_w_down[i])
        xp = xp + yp
        xs = xs + ys
        cvp.append(sp); cvs.append(ss)
    return (xp, xs,
            jnp.stack(fkp), jnp.stack(fvp), jnp.stack(flp),
            jnp.stack(mkp), jnp.stack(mvp), jnp.stack(cvp),
            jnp.stack(fks), jnp.stack(fvs), jnp.stack(fls),
            jnp.stack(mks), jnp.stack(mvs), jnp.stack(cvs))
```

```python
import functools

import jax
import jax.numpy as jnp
from jax import lax
from jax.experimental import pallas as pl
from jax.experimental.pallas import tpu as pltpu

D_MODEL = 2048
N_HEADS = 16
HEAD_DIM = D_MODEL // N_HEADS
D_FF = 5632
CONV_W = 3
ROPE_THETA = 10000.0
MOBA_BLOCK = 256
MOBA_TOPK = 3
PAGE_SIZE = 128
EPS = 1e-6
ATTN_SCALE = HEAD_DIM ** -0.5

LANES = 128
HALO = 16
PAGES_PER_STEP = 4
VMEM_LIMIT = 52 * 1024 * 1024

F32 = jnp.float32
BF16 = jnp.bfloat16
HIGHEST = lax.Precision.HIGHEST
NT_DIMS = (((1,), (1,)), ((), ()))


def _params(*sem):
    return pltpu.CompilerParams(dimension_semantics=sem, vmem_limit_bytes=VMEM_LIMIT)


def _rms_rows(x, g):
    return x * lax.rsqrt(jnp.mean(x * x, axis=-1, keepdims=True) + EPS) * g


def _proj_kernel(*refs, head_norm, rope, emit_bf16):
    it = iter(refs)
    x_ref, gn_ref, w_ref = next(it), next(it), next(it)
    gain_ref = next(it) if head_norm else None
    cos_ref, sin_ref = (next(it), next(it)) if rope else (None, None)
    y32_ref = next(it)
    y16_ref = next(it) if emit_bf16 else None
    h_scr = next(it)

    @pl.when(pl.program_id(1) == 0)
    def _():
        h_scr[...] = _rms_rows(x_ref[...], gn_ref[...]).astype(BF16)

    acc = jnp.dot(h_scr[...], w_ref[...], preferred_element_type=F32)
    for c in range(acc.shape[1] // HEAD_DIM):
        sl = slice(c * HEAD_DIM, (c + 1) * HEAD_DIM)
        y = acc[:, sl]
        if head_norm:
            y = _rms_rows(y, gain_ref[...])
        if rope:
            y = y * cos_ref[...] + pltpu.roll(y, HEAD_DIM // 2, 1) * sin_ref[...]
        y32_ref[:, sl] = y
        if emit_bf16:
            y16_ref[:, sl] = y.astype(BF16)


def _proj(x, gn, w16, layer, col0, *, gain=None, cos=None, sin=None, emit_bf16=False, tm, tn=512):
    m = x.shape[0]
    head_norm, rope = gain is not None, cos is not None
    grid = (m // tm, D_MODEL // tn)
    in_specs = [pl.BlockSpec((tm, D_MODEL), lambda i, n: (i, 0)),
                pl.BlockSpec((1, D_MODEL), lambda i, n: (0, 0)),
                pl.BlockSpec((None, D_MODEL, tn), lambda i, n: (layer, 0, col0 // tn + n))]
    args = [x, gn.reshape(1, D_MODEL), w16]
    if head_norm:
        in_specs.append(pl.BlockSpec((1, HEAD_DIM), lambda i, n: (0, 0)))
        args.append(gain.reshape(1, HEAD_DIM))
    if rope:
        nblk = cos.shape[0] // tm
        in_specs += [pl.BlockSpec((tm, HEAD_DIM), lambda i, n: (i % nblk, 0))] * 2
        args += [cos, sin]
    out_shape = [jax.ShapeDtypeStruct((m, D_MODEL), F32)]
    out_specs = [pl.BlockSpec((tm, tn), lambda i, n: (i, n))]
    if emit_bf16:
        out_shape.append(jax.ShapeDtypeStruct((m, D_MODEL), BF16))
        out_specs.append(pl.BlockSpec((tm, tn), lambda i, n: (i, n)))
    out = pl.pallas_call(
        functools.partial(_proj_kernel, head_norm=head_norm, rope=rope, emit_bf16=emit_bf16),
        grid=grid, in_specs=in_specs, out_specs=out_specs, out_shape=out_shape,
        scratch_shapes=[pltpu.VMEM((tm, D_MODEL), BF16)],
        compiler_params=_params("arbitrary", "arbitrary"), name="proj")(*args)
    return out if emit_bf16 else out[0]


def _logf_kernel(x_ref, gn_ref, w_ref, b_ref, o_ref):
    h = _rms_rows(x_ref[...], gn_ref[...]).astype(BF16)
    z = jnp.dot(h, w_ref[...], preferred_element_type=F32) + b_ref[...]
    o_ref[...] = jnp.minimum(z, 0.0) - jnp.log1p(jnp.exp(-jnp.abs(z)))


def _logf_proj(x, gn, wf16, bf, *, tm):
    m = x.shape[0]
    return pl.pallas_call(
        _logf_kernel, grid=(m // tm,),
        in_specs=[pl.BlockSpec((tm, D_MODEL), lambda i: (i, 0)),
                  pl.BlockSpec((1, D_MODEL), lambda i: (0, 0)),
                  pl.BlockSpec((D_MODEL, N_HEADS), lambda i: (0, 0)),
                  pl.BlockSpec((1, N_HEADS), lambda i: (0, 0))],
        out_specs=pl.BlockSpec((tm, N_HEADS), lambda i: (i, 0)),
        out_shape=jax.ShapeDtypeStruct((m, N_HEADS), F32),
        compiler_params=_params("arbitrary"), name="logf_proj")(
            x, gn.reshape(1, D_MODEL), wf16, bf.reshape(1, N_HEADS))


def _upper_ones(n):
    r = lax.broadcasted_iota(jnp.int32, (n, n), 0)
    c = lax.broadcasted_iota(jnp.int32, (n, n), 1)
    return (r <= c).astype(F32)


def _cumsum_rows_kernel(x_ref, o_ref, *, tk):
    u = _upper_ones(tk)
    carry = jnp.zeros((x_ref.shape[0], 1), F32)
    for s in range(x_ref.shape[1] // tk):
        c = jnp.dot(x_ref[:, s * tk:(s + 1) * tk], u, precision=HIGHEST, preferred_element_type=F32) + carry
        o_ref[:, s * tk:(s + 1) * tk] = c
        carry = c[:, tk - 1:tk]


def _cumsum_rows(x):
    return pl.pallas_call(
        functools.partial(_cumsum_rows_kernel, tk=512),
        out_shape=jax.ShapeDtypeStruct(x.shape, F32), name="cumsum_rows")(x)


def _block_mean_kernel(k_ref, o_ref):
    o_ref[0] = jnp.mean(k_ref[...], axis=0, keepdims=True)


def _block_mean(k32):
    nb = k32.shape[0] // MOBA_BLOCK
    return pl.pallas_call(
        _block_mean_kernel, grid=(nb,),
        in_specs=[pl.BlockSpec((MOBA_BLOCK, D_MODEL), lambda i: (i, 0))],
        out_specs=pl.BlockSpec((1, 1, D_MODEL), lambda i: (i, 0, 0)),
        out_shape=jax.ShapeDtypeStruct((nb, 1, D_MODEL), F32),
        compiler_params=_params("arbitrary"), name="block_mean")(k32)


def _topk_mask(gate, valid, k):
    idx = lax.broadcasted_iota(jnp.int32, gate.shape, 1)
    n = gate.shape[1]
    g = jnp.where(valid, gate, -jnp.inf)
    sel = jnp.zeros(gate.shape, F32)
    for _ in range(k):
        mx = jnp.max(g, axis=1, keepdims=True)
        is_max = (g == mx) & (g > -jnp.inf)
        first = jnp.min(jnp.where(is_max, idx, n), axis=1, keepdims=True)
        pick = idx == first
        sel = jnp.where(pick, 1.0, sel)
        g = jnp.where(pick, -jnp.inf, g)
    return sel


def _column(mat, j):
    idx = lax.broadcasted_iota(jnp.int32, mat.shape, 1)
    return jnp.sum(jnp.where(idx == j, mat, 0.0), axis=1, keepdims=True)


def _prompt_attn_kernel(q_ref, k_ref, v_ref, aux_ref, o_ref, *, mode, tq):
    qi = pl.program_id(2)
    q = q_ref[...]
    qs = (q * ATTN_SCALE).astype(BF16)
    row = lax.broadcasted_iota(jnp.int32, (tq, tq), 0)
    col = lax.broadcasted_iota(jnp.int32, (tq, tq), 1)
    if mode == "fox":
        q0 = pl.multiple_of(qi * tq, tq)
        cq_row = aux_ref[0, :, pl.ds(q0, tq)]
        cq = jnp.sum(jnp.where(row == col, jnp.broadcast_to(cq_row, (tq, tq)), 0.0), axis=1, keepdims=True)
        sel = None
    else:
        kmean = aux_ref[0]
        gate = lax.dot_general(q, kmean, NT_DIMS, precision=HIGHEST, preferred_element_type=F32)
        blk = lax.broadcasted_iota(jnp.int32, gate.shape, 1)
        sel = _topk_mask(gate, blk < qi, MOBA_TOPK)

    def step(j, carry, diag):
        m, l, acc = carry
        k0 = pl.multiple_of(j * tq, tq)
        s = lax.dot_general(qs, k_ref[pl.ds(k0, tq), :], NT_DIMS, preferred_element_type=F32)
        if mode == "fox":
            s = s + (cq - aux_ref[0, :, pl.ds(k0, tq)])
        if diag:
            s = jnp.where(col <= row, s, -jnp.inf)
        elif mode == "moba":
            s = jnp.where(_column(sel, j) > 0.5, s, -jnp.inf)
        m_new = jnp.maximum(m, jnp.max(s, axis=1, keepdims=True))
        m_safe = jnp.where(m_new == -jnp.inf, 0.0, m_new)
        alpha = jnp.exp(m - m_safe)
        p = jnp.exp(s - m_safe)
        l = alpha * l + jnp.sum(p, axis=1, keepdims=True)
        acc = alpha * acc + jnp.dot(p.astype(BF16), v_ref[pl.ds(k0, tq), :], preferred_element_type=F32)
        return m_new, l, acc

    init = (jnp.full((tq, 1), -jnp.inf, F32), jnp.zeros((tq, 1), F32), jnp.zeros((tq, HEAD_DIM), F32))
    carry = lax.fori_loop(0, qi, lambda j, c: step(j, c, False), init)
    _, l, acc = step(qi, carry, True)
    o_ref[...] = (acc / l).astype(BF16)


def _prompt_attn(q32, k16, v16, aux, *, mode, batch, seq, tq=MOBA_BLOCK):
    nq = seq // tq
    if mode == "fox":
        aux_spec = pl.BlockSpec((1, 1, seq), lambda b, h, i: (b * N_HEADS + h, 0, 0))
    else:
        aux_spec = pl.BlockSpec((1, seq // MOBA_BLOCK, HEAD_DIM), lambda b, h, i: (b, 0, h))
    return pl.pallas_call(
        functools.partial(_prompt_attn_kernel, mode=mode, tq=tq),
        grid=(batch, N_HEADS, nq),
        in_specs=[pl.BlockSpec((tq, HEAD_DIM), lambda b, h, i: (b * nq + i, h)),
                  pl.BlockSpec((seq, HEAD_DIM), lambda b, h, i: (b, h)),
                  pl.BlockSpec((seq, HEAD_DIM), lambda b, h, i: (b, h)),
                  aux_spec],
        out_specs=pl.BlockSpec((tq, HEAD_DIM), lambda b, h, i: (b * nq + i, h)),
        out_shape=jax.ShapeDtypeStruct((batch * seq, D_MODEL), BF16),
        compiler_params=_params("arbitrary", "arbitrary", "arbitrary"), name=f"prompt_attn_{mode}")(
            q32, k16, v16, aux)


def _sample_attn_kernel(*refs, mode, n_groups, t_new):
    pp = PAGES_PER_STEP
    it = iter(refs)
    next(it)
    q_ref, kn_ref, vn_ref = next(it), next(it), next(it)
    lfn_ref = next(it) if mode == "fox" else None
    k_refs = [next(it) for _ in range(pp)]
    v_refs = [next(it) for _ in range(pp)]
    lf_refs = [next(it) for _ in range(pp)] if mode == "fox" else None
    o_ref = next(it)
    s_scr, p_scr, acc_scr, l_scr, qbd16_scr, aux_scr = (next(it) for _ in range(6))

    ph, g = pl.program_id(1), pl.program_id(2)
    n_rows = N_HEADS * t_new
    past = n_groups * pp * PAGE_SIZE
    n_chunks = past // LANES + 1
    row = lax.broadcasted_iota(jnp.int32, (n_rows, LANES), 0)
    lane = lax.broadcasted_iota(jnp.int32, (n_rows, LANES), 1)
    tok = row % t_new
    if mode == "fox":
        hh = lax.broadcasted_iota(jnp.int32, (n_rows, N_HEADS), 1)
        rr = lax.broadcasted_iota(jnp.int32, (n_rows, N_HEADS), 0)
        expand = (rr // t_new == hh).astype(F32)
        upper = _upper_ones(LANES)

    def pad_rows(x):
        return jnp.concatenate([x, jnp.zeros((LANES - x.shape[0], x.shape[1]), x.dtype)], axis=0)

    def neg_cum_bias(lf, carry):
        x = lax.dot_general(expand, lf, NT_DIMS, precision=HIGHEST, preferred_element_type=F32)
        return jnp.dot(x, upper, precision=HIGHEST, preferred_element_type=F32) + carry

    @pl.when((ph == 0) & (g == 0))
    def _():
        q = q_ref[0]
        qt = jnp.concatenate([q] * N_HEADS, axis=0)
        r = lax.broadcasted_iota(jnp.int32, qt.shape, 0)
        c = lax.broadcasted_iota(jnp.int32, qt.shape, 1)
        qbd = jnp.where(c // HEAD_DIM == r // t_new, qt, 0.0)
        qbd16_scr[...] = (qbd * ATTN_SCALE).astype(BF16)
        aux_scr[...] = jnp.zeros(aux_scr.shape, F32)

    @pl.when(ph == 0)
    def _():
        for i in range(pp):
            page = g * pp + i
            kf = k_refs[i][0]
            s = lax.dot_general(qbd16_scr[...], kf.astype(BF16), NT_DIMS, preferred_element_type=F32)
            if mode == "fox":
                cks = neg_cum_bias(lf_refs[i][0], aux_scr[...])
                aux_scr[...] = cks[:, LANES - 1:LANES]
                s = s - cks
            else:
                blk = page // (MOBA_BLOCK // PAGE_SIZE)
                aux_scr[pl.ds(blk, 1), :] += jnp.sum(kf, axis=0, keepdims=True)
            s_scr[:, pl.ds(pl.multiple_of(page * PAGE_SIZE, PAGE_SIZE), PAGE_SIZE)] = s

    @pl.when((ph == 0) & (g == n_groups - 1))
    def _():
        s_new = lax.dot_general(qbd16_scr[...], pad_rows(kn_ref[0]).astype(BF16), NT_DIMS,
                                preferred_element_type=F32)
        if mode == "fox":
            cks = neg_cum_bias(pad_rows(lfn_ref[0]), aux_scr[...])
            cq = jnp.sum(jnp.where(lane == tok, cks, 0.0), axis=1, keepdims=True)
            s_new = s_new - cks + cq
        s_scr[:, past:past + LANES] = jnp.where(lane <= tok, s_new, -jnp.inf)

        if mode == "fox":
            def add_cq(c, _):
                sl = pl.ds(pl.multiple_of(c * LANES, LANES), LANES)
                s_scr[:, sl] = s_scr[:, sl] + cq
                return 0
            lax.fori_loop(0, past // LANES, add_cq, 0)
        else:
            q = q_ref[0]
            qt = jnp.concatenate([q] * N_HEADS, axis=0)
            r = lax.broadcasted_iota(jnp.int32, qt.shape, 0)
            c = lax.broadcasted_iota(jnp.int32, qt.shape, 1)
            qbd = jnp.where(c // HEAD_DIM == r // t_new, qt, 0.0)
            kmean = aux_scr[...] / MOBA_BLOCK
            gate = lax.dot_general(qbd, kmean, NT_DIMS, precision=HIGHEST, preferred_element_type=F32)
            sel = _topk_mask(gate, gate == gate, MOBA_TOPK)

            def mask_block(b, _):
                sl = pl.ds(pl.multiple_of(b * MOBA_BLOCK, MOBA_BLOCK), MOBA_BLOCK)
                s_scr[:, sl] = jnp.where(_column(sel, b) > 0.5, s_scr[:, sl], -jnp.inf)
                return 0
            lax.fori_loop(0, past // MOBA_BLOCK, mask_block, 0)

        def run_max(c, m):
            return jnp.maximum(m, s_scr[:, pl.ds(pl.multiple_of(c * LANES, LANES), LANES)])
        m = lax.fori_loop(0, n_chunks, run_max, jnp.full((n_rows, LANES), -jnp.inf, F32))
        m = jnp.max(m, axis=1, keepdims=True)

        def run_exp(c, l):
            sl = pl.ds(pl.multiple_of(c * LANES, LANES), LANES)
            p = jnp.exp(s_scr[:, sl] - m)
            p_scr[:, sl] = p.astype(BF16)
            return l + p
        l = lax.fori_loop(0, n_chunks, run_exp, jnp.zeros((n_rows, LANES), F32))
        l_scr[...] = jnp.sum(l, axis=1, keepdims=True)
        acc_scr[...] = jnp.zeros(acc_scr.shape, F32)

    @pl.when(ph == 1)
    def _():
        for i in range(pp):
            page = g * pp + i
            p = p_scr[:, pl.ds(pl.multiple_of(page * PAGE_SIZE, PAGE_SIZE), PAGE_SIZE)]
            acc_scr[...] += jnp.dot(p, v_refs[i][0].astype(BF16), preferred_element_type=F32)

    @pl.when((ph == 1) & (g == n_groups - 1))
    def _():
        o = acc_scr[...] + jnp.dot(p_scr[:, past:past + LANES], pad_rows(vn_ref[0]).astype(BF16),
                                   preferred_element_type=F32)
        o = o / l_scr[...]
        for h in range(N_HEADS):
            o_ref[0, :, h * HEAD_DIM:(h + 1) * HEAD_DIM] = (
                o[h * t_new:(h + 1) * t_new, h * HEAD_DIM:(h + 1) * HEAD_DIM].astype(BF16))


def _sample_attn(q32, k32, v32, lf_new, cache_k, cache_v, cache_lf, page_table, slot, n_pool, *, mode):
    bs, n_pages = page_table.shape
    t_new = q32.shape[0] // bs
    pp = PAGES_PER_STEP
    n_groups = n_pages // pp
    base = slot * n_pool
    n_rows = N_HEADS * t_new
    past = n_pages * PAGE_SIZE

    def k_map(i):
        return lambda b, ph, g, pt: (base + pt[b, jnp.where(ph == 0, g, n_groups - 1) * pp + i], 0, 0)

    def v_map(i):
        return lambda b, ph, g, pt: (base + pt[b, jnp.where(ph == 0, 0, g) * pp + i], 0, 0)

    new_spec = pl.BlockSpec((1, t_new, D_MODEL), lambda b, ph, g, pt: (b, 0, 0))
    in_specs = [new_spec, new_spec, new_spec]
    args = [q32.reshape(bs, t_new, D_MODEL), k32.reshape(bs, t_new, D_MODEL), v32.reshape(bs, t_new, D_MODEL)]
    if mode == "fox":
        in_specs.append(pl.BlockSpec((1, t_new, N_HEADS), lambda b, ph, g, pt: (b, 0, 0)))
        args.append(lf_new.reshape(bs, t_new, N_HEADS))
    in_specs += [pl.BlockSpec((1, PAGE_SIZE, D_MODEL), k_map(i)) for i in range(pp)]
    args += [cache_k] * pp
    in_specs += [pl.BlockSpec((1, PAGE_SIZE, D_MODEL), v_map(i)) for i in range(pp)]
    args += [cache_v] * pp
    if mode == "fox":
        in_specs += [pl.BlockSpec((1, PAGE_SIZE, N_HEADS), k_map(i)) for i in range(pp)]
        args += [cache_lf] * pp
        aux_shape = (n_rows, 1)
    else:
        aux_shape = (past // MOBA_BLOCK, D_MODEL)
    out = pl.pallas_call(
        functools.partial(_sample_attn_kernel, mode=mode, n_groups=n_groups, t_new=t_new),
        grid_spec=pltpu.PrefetchScalarGridSpec(
            num_scalar_prefetch=1, grid=(bs, 2, n_groups), in_specs=in_specs,
            out_specs=pl.BlockSpec((1, t_new, D_MODEL), lambda b, ph, g, pt: (b, 0, 0)),
            scratch_shapes=[pltpu.VMEM((n_rows, past + LANES), F32),
                            pltpu.VMEM((n_rows, past + LANES), BF16),
                            pltpu.VMEM((n_rows, D_MODEL), F32),
                            pltpu.VMEM((n_rows, 1), F32),
                            pltpu.VMEM((n_rows, D_MODEL), BF16),
                            pltpu.VMEM(aux_shape, F32)]),
        out_shape=jax.ShapeDtypeStruct((bs, t_new, D_MODEL), BF16),
        compiler_params=_params("arbitrary", "arbitrary", "arbitrary"), name=f"sample_attn_{mode}")(
            page_table, *args)
    return out.reshape(bs * t_new, D_MODEL)


def _oproj_kernel(o_ref, w_ref, x_ref, y_ref):
    y_ref[...] = x_ref[...] + jnp.dot(o_ref[...], w_ref[...], preferred_element_type=F32)


def _oproj(o16, w16, layer, x, *, tm, tn=512):
    m = x.shape[0]
    return pl.pallas_call(
        _oproj_kernel, grid=(m // tm, D_MODEL // tn),
        in_specs=[pl.BlockSpec((tm, D_MODEL), lambda i, n: (i, 0)),
                  pl.BlockSpec((None, D_MODEL, tn), lambda i, n: (layer, 0, n)),
                  pl.BlockSpec((tm, tn), lambda i, n: (i, n))],
        out_specs=pl.BlockSpec((tm, tn), lambda i, n: (i, n)),
        out_shape=jax.ShapeDtypeStruct((m, D_MODEL), F32),
        compiler_params=_params("arbitrary", "arbitrary"), name="oproj")(o16, w16, x)


def _conv_gate(gx_scr, g, cw_ref, cb_ref, g1_fix=None, g2_fix=None):
    tm = g.shape[0]
    gx_scr[HALO:HALO + tm, :] = g
    g1 = gx_scr[HALO - 1:HALO - 1 + tm, :]
    g2 = gx_scr[HALO - 2:HALO - 2 + tm, :]
    if g1_fix is not None:
        g1, g2 = g1_fix(g1), g2_fix(g2)
    acc = cb_ref[...] + g2 * cw_ref[0:1, :]
    acc = acc + g1 * cw_ref[1:2, :]
    return acc + g * cw_ref[2:3, :]


def _ffn_prompt_kernel(x_ref, xp_ref, gn_ref, wg_ref, wu_ref, cw_ref, cb_ref, wd_ref, y_ref, gt_ref,
                       h_scr, hp_scr, acc_scr, gx_scr, *, tiles_per_seq):
    i, f = pl.program_id(0), pl.program_id(1)
    tm = x_ref.shape[0]

    @pl.when(f == 0)
    def _():
        h_scr[...] = _rms_rows(x_ref[...], gn_ref[...]).astype(BF16)
        hp_scr[...] = _rms_rows(xp_ref[...], gn_ref[...]).astype(BF16)
        acc_scr[...] = jnp.zeros(acc_scr.shape, F32)

    g = jnp.dot(h_scr[...], wg_ref[...], preferred_element_type=F32)
    u = jnp.dot(h_scr[...], wu_ref[...], preferred_element_type=F32)
    g_prev = jnp.dot(hp_scr[...], wg_ref[...], preferred_element_type=F32)
    gx_scr[0:HALO, :] = jnp.where(i % tiles_per_seq == 0, 0.0, g_prev)
    a = _conv_gate(gx_scr, g, cw_ref, cb_ref)
    a = (a * jax.nn.sigmoid(a)) * u
    acc_scr[...] += jnp.dot(a.astype(BF16), wd_ref[...], preferred_element_type=F32)
    gt_ref[0] = g[tm - 8:tm, :]

    @pl.when(f == pl.num_programs(1) - 1)
    def _():
        y_ref[...] = x_ref[...] + acc_scr[...]


def _ffn_prompt(x, gn, wg16, wu16, cw, cb, wd16, layer, *, seq, tm=512, tf=512):
    m = x.shape[0]
    nf = D_FF // tf
    hb = tm // HALO
    y, gt = pl.pallas_call(
        functools.partial(_ffn_prompt_kernel, tiles_per_seq=seq // tm),
        grid=(m // tm, nf),
        in_specs=[pl.BlockSpec((tm, D_MODEL), lambda i, f: (i, 0)),
                  pl.BlockSpec((HALO, D_MODEL), lambda i, f: (jnp.maximum(i * hb - 1, 0), 0)),
                  pl.BlockSpec((1, D_MODEL), lambda i, f: (0, 0)),
                  pl.BlockSpec((None, D_MODEL, tf), lambda i, f: (layer, 0, f)),
                  pl.BlockSpec((None, D_MODEL, tf), lambda i, f: (layer, 0, f)),
                  pl.BlockSpec((None, CONV_W, tf), lambda i, f: (layer, 0, f)),
                  pl.BlockSpec((None, 1, tf), lambda i, f: (layer, 0, f)),
                  pl.BlockSpec((None, tf, D_MODEL), lambda i, f: (layer, f, 0))],
        out_specs=[pl.BlockSpec((tm, D_MODEL), lambda i, f: (i, 0)),
                   pl.BlockSpec((1, 8, tf), lambda i, f: (i, 0, f))],
        out_shape=[jax.ShapeDtypeStruct((m, D_MODEL), F32),
                   jax.ShapeDtypeStruct((m // tm, 8, D_FF), F32)],
        scratch_shapes=[pltpu.VMEM((tm, D_MODEL), BF16), pltpu.VMEM((HALO, D_MODEL), BF16),
                        pltpu.VMEM((tm, D_MODEL), F32), pltpu.VMEM((tm + HALO, tf), F32)],
        compiler_params=_params("arbitrary", "arbitrary"), name="ffn_prompt")(
            x, x, gn.reshape(1, D_MODEL), wg16, wu16, cw, cb.reshape(cb.shape[0], 1, D_FF), wd16)
    tiles = seq // tm
    tail = gt.reshape(m // seq, tiles, 8, D_FF)[:, -1, 8 - (CONV_W - 1):, :]
    return y, tail


def _ffn_sample_kernel(x_ref, gn_ref, wg_ref, wu_ref, cw_ref, cb_ref, wd_ref, s1_ref, s2_ref, y_ref, g_ref,
                       h_scr, acc_scr, gx_scr, *, t_new):
    f = pl.program_id(0)

    @pl.when(f == 0)
    def _():
        h_scr[...] = _rms_rows(x_ref[...], gn_ref[...]).astype(BF16)
        acc_scr[...] = jnp.zeros(acc_scr.shape, F32)
        gx_scr[...] = jnp.zeros(gx_scr.shape, F32)

    g = jnp.dot(h_scr[...], wg_ref[...], preferred_element_type=F32)
    u = jnp.dot(h_scr[...], wu_ref[...], preferred_element_type=F32)
    tok = lax.broadcasted_iota(jnp.int32, g.shape, 0) % t_new
    a = _conv_gate(gx_scr, g, cw_ref, cb_ref,
                   g1_fix=lambda g1: jnp.where(tok >= 1, g1, s1_ref[...]),
                   g2_fix=lambda g2: jnp.where(tok >= 2, g2, s2_ref[...]))
    a = (a * jax.nn.sigmoid(a)) * u
    acc_scr[...] += jnp.dot(a.astype(BF16), wd_ref[...], preferred_element_type=F32)
    g_ref[...] = g

    @pl.when(f == pl.num_programs(0) - 1)
    def _():
        y_ref[...] = x_ref[...] + acc_scr[...]


def _ffn_sample(x, gn, wg16, wu16, cw, cb, wd16, layer, state, *, tf=512):
    m = x.shape[0]
    bs = state.shape[0]
    t_new = m // bs
    s1 = jnp.broadcast_to(state[:, -1:, :], (bs, t_new, D_FF)).reshape(m, D_FF)
    s2 = jnp.tile(state, (1, t_new // (CONV_W - 1), 1)).reshape(m, D_FF)
    y, g = pl.pallas_call(
        functools.partial(_ffn_sample_kernel, t_new=t_new),
        grid=(D_FF // tf,),
        in_specs=[pl.BlockSpec((m, D_MODEL), lambda f: (0, 0)),
                  pl.BlockSpec((1, D_MODEL), lambda f: (0, 0)),
                  pl.BlockSpec((None, D_MODEL, tf), lambda f: (layer, 0, f)),
                  pl.BlockSpec((None, D_MODEL, tf), lambda f: (layer, 0, f)),
                  pl.BlockSpec((None, CONV_W, tf), lambda f: (layer, 0, f)),
                  pl.BlockSpec((None, 1, tf), lambda f: (layer, 0, f)),
                  pl.BlockSpec((None, tf, D_MODEL), lambda f: (layer, f, 0)),
                  pl.BlockSpec((m, tf), lambda f: (0, f)),
                  pl.BlockSpec((m, tf), lambda f: (0, f))],
        out_specs=[pl.BlockSpec((m, D_MODEL), lambda f: (0, 0)),
                   pl.BlockSpec((m, tf), lambda f: (0, f))],
        out_shape=[jax.ShapeDtypeStruct((m, D_MODEL), F32), jax.ShapeDtypeStruct((m, D_FF), F32)],
        scratch_shapes=[pltpu.VMEM((m, D_MODEL), BF16), pltpu.VMEM((m, D_MODEL), F32),
                        pltpu.VMEM((m + HALO, tf), F32)],
        compiler_params=_params("arbitrary"), name="ffn_sample")(
            x, gn.reshape(1, D_MODEL), wg16, wu16, cw, cb.reshape(cb.shape[0], 1, D_FF), wd16, s1, s2)
    return y, g.reshape(bs, t_new, D_FF)[:, t_new - (CONV_W - 1):, :]


def _rope_tables(pos):
    half = HEAD_DIM // 2
    inv_freq = ROPE_THETA ** (-jnp.arange(half, dtype=F32) / half)
    ang = pos.astype(F32)[:, None] * inv_freq[None, :]
    cos, sin = jnp.cos(ang), jnp.sin(ang)
    return jnp.concatenate([cos, cos], axis=-1), jnp.concatenate([-sin, sin], axis=-1)


def kernel(x_prompt, x_sample, cache_fox_k, cache_fox_v, cache_fox_logf, cache_moba_k, cache_moba_v,
           state_ffn_conv, page_table, attn_norm, ffn_norm, fox_w_in, fox_b_f, fox_q_norm, fox_k_norm, fox_w_o,
           moba_w_in, moba_q_norm, moba_k_norm, moba_w_o, ffn_w_gate, ffn_w_up, ffn_conv_w, ffn_conv_b,
           ffn_w_down):
    B, T, _ = x_prompt.shape
    Bs, Ts, _ = x_sample.shape
    depth = attn_norm.shape[0]
    n_fox, n_pool = cache_fox_k.shape[:2]
    n_moba = cache_moba_k.shape[0]
    past_len = page_table.shape[1] * PAGE_SIZE
    mp, ms = B * T, Bs * Ts
    tm_p = 1024

    fox_w16 = fox_w_in.astype(BF16)
    fox_wf16 = fox_w16[:, :, 3 * D_MODEL:]
    moba_w16 = moba_w_in.astype(BF16)
    fox_wo16, moba_wo16 = fox_w_o.astype(BF16), moba_w_o.astype(BF16)
    wg16, wu16, wd16 = ffn_w_gate.astype(BF16), ffn_w_up.astype(BF16), ffn_w_down.astype(BF16)

    fk = cache_fox_k.reshape(n_fox * n_pool, PAGE_SIZE, D_MODEL)
    fv = cache_fox_v.reshape(n_fox * n_pool, PAGE_SIZE, D_MODEL)
    fl = cache_fox_logf.reshape(n_fox * n_pool, PAGE_SIZE, N_HEADS)
    mk = cache_moba_k.reshape(n_moba * n_pool, PAGE_SIZE, D_MODEL)
    mv = cache_moba_v.reshape(n_moba * n_pool, PAGE_SIZE, D_MODEL)

    cos_p, sin_p = _rope_tables(jnp.arange(T, dtype=jnp.int32))
    cos_s, sin_s = _rope_tables(past_len + jnp.arange(Ts, dtype=jnp.int32))
    cos_s, sin_s = jnp.tile(cos_s, (Bs, 1)), jnp.tile(sin_s, (Bs, 1))

    xp, xs = x_prompt.reshape(mp, D_MODEL), x_sample.reshape(ms, D_MODEL)
    outs = {k: [] for k in ("fkp", "fvp", "flp", "fks", "fvs", "fls", "mkp", "mvp", "mks", "mvs", "cvp", "cvs")}
    heads_p = (B, T, N_HEADS, HEAD_DIM)
    heads_s = (Bs, Ts, N_HEADS, HEAD_DIM)
    for i in range(depth):
        j = i // 2
        if i % 2 == 0:
            w16, gq, gk = fox_w16, fox_q_norm[j], fox_k_norm[j]
            qp = _proj(xp, attn_norm[i], w16, j, 0, gain=gq, tm=tm_p)
            kp, kp16 = _proj(xp, attn_norm[i], w16, j, D_MODEL, gain=gk, emit_bf16=True, tm=tm_p)
            vp, vp16 = _proj(xp, attn_norm[i], w16, j, 2 * D_MODEL, emit_bf16=True, tm=tm_p)
            lp = _logf_proj(xp, attn_norm[i], fox_wf16[j], fox_b_f[j], tm=tm_p)
            qs = _proj(xs, attn_norm[i], w16, j, 0, gain=gq, tm=ms)
            ks = _proj(xs, attn_norm[i], w16, j, D_MODEL, gain=gk, tm=ms)
            vs = _proj(xs, attn_norm[i], w16, j, 2 * D_MODEL, tm=ms)
            ls = _logf_proj(xs, attn_norm[i], fox_wf16[j], fox_b_f[j], tm=ms)
            lp_rows = lp.reshape(B, T, N_HEADS).transpose(0, 2, 1).reshape(B * N_HEADS, T)
            c_rows = _cumsum_rows(lp_rows).reshape(B * N_HEADS, 1, T)
            op = _prompt_attn(qp, kp16, vp16, c_rows, mode="fox", batch=B, seq=T)
            os_ = _sample_attn(qs, ks, vs, ls, fk, fv, fl, page_table, j, n_pool, mode="fox")
            wo16 = fox_wo16
            outs["fkp"].append(kp.reshape(heads_p)); outs["fvp"].append(vp.reshape(heads_p))
            outs["flp"].append(lp.reshape(B, T, N_HEADS))
            outs["fks"].append(ks.reshape(heads_s)); outs["fvs"].append(vs.reshape(heads_s))
            outs["fls"].append(ls.reshape(Bs, Ts, N_HEADS))
        else:
            w16, gq, gk = moba_w16, moba_q_norm[j], moba_k_norm[j]
            qp = _proj(xp, attn_norm[i], w16, j, 0, gain=gq, cos=cos_p, sin=sin_p, tm=tm_p)
            kp, kp16 = _proj(xp, attn_norm[i], w16, j, D_MODEL, gain=gk, cos=cos_p, sin=sin_p,
                             emit_bf16=True, tm=tm_p)
            vp, vp16 = _proj(xp, attn_norm[i], w16, j, 2 * D_MODEL, emit_bf16=True, tm=tm_p)
            qs = _proj(xs, attn_norm[i], w16, j, 0, gain=gq, cos=cos_s, sin=sin_s, tm=ms)
            ks = _proj(xs, attn_norm[i], w16, j, D_MODEL, gain=gk, cos=cos_s, sin=sin_s, tm=ms)
            vs = _proj(xs, attn_norm[i], w16, j, 2 * D_MODEL, tm=ms)
            kmean = _block_mean(kp).reshape(B, T // MOBA_BLOCK, D_MODEL)
            op = _prompt_attn(qp, kp16, vp16, kmean, mode="moba", batch=B, seq=T)
            os_ = _sample_attn(qs, ks, vs, None, mk, mv, None, page_table, j, n_pool, mode="moba")
            wo16 = moba_wo16
            outs["mkp"].append(kp.reshape(heads_p)); outs["mvp"].append(vp.reshape(heads_p))
            outs["mks"].append(ks.reshape(heads_s)); outs["mvs"].append(vs.reshape(heads_s))
        xp = _oproj(op, wo16, j, xp, tm=tm_p)
        xs = _oproj(os_, wo16, j, xs, tm=ms)
        xp, sp = _ffn_prompt(xp, ffn_norm[i], wg16, wu16, ffn_conv_w, ffn_conv_b, wd16, i, seq=T)
        xs, ss = _ffn_sample(xs, ffn_norm[i], wg16, wu16, ffn_conv_w, ffn_conv_b, wd16, i, state_ffn_conv[i])
        outs["cvp"].append(sp); outs["cvs"].append(ss)
    st = lambda k: jnp.stack(outs[k])
    return (xp.reshape(B, T, D_MODEL), xs.reshape(Bs, Ts, D_MODEL),
            st("fkp"), st("fvp"), st("flp"), st("mkp"), st("mvp"), st("cvp"),
            st("fks"), st("fvs"), st("fls"), st("mks"), st("mvs"), st("cvs"))
```

```python
import functools

import jax
import jax.numpy as jnp
from jax import lax
from jax.experimental import pallas as pl
from jax.experimental.pallas import tpu as pltpu

D_MODEL = 2048
N_HEADS = 16
HEAD_DIM = D_MODEL // N_HEADS
D_FF = 5632
CONV_W = 3
ROPE_THETA = 10000.0
MOBA_BLOCK = 256
MOBA_TOPK = 3
PAGE_SIZE = 128
EPS = 1e-6
ATTN_SCALE = HEAD_DIM ** -0.5

LANES = 128
HALO = 16
PAGES_PER_STEP = 4
VMEM_LIMIT = 52 * 1024 * 1024

F32 = jnp.float32
BF16 = jnp.bfloat16
HIGHEST = lax.Precision.HIGHEST
NT_DIMS = (((1,), (1,)), ((), ()))


def _params(*sem):
    return pltpu.CompilerParams(dimension_semantics=sem, vmem_limit_bytes=VMEM_LIMIT)


def _rms_rows(x, g):
    return x * lax.rsqrt(jnp.mean(x * x, axis=-1, keepdims=True) + EPS) * g


def _proj_kernel(*refs, head_norm, rope, emit_bf16):
    it = iter(refs)
    x_ref, gn_ref, w_ref = next(it), next(it), next(it)
    gain_ref = next(it) if head_norm else None
    cos_ref, sin_ref = (next(it), next(it)) if rope else (None, None)
    y32_ref = next(it)
    y16_ref = next(it) if emit_bf16 else None
    h_scr = next(it)

    @pl.when(pl.program_id(1) == 0)
    def _():
        h_scr[...] = _rms_rows(x_ref[...], gn_ref[...]).astype(BF16)

    acc = jnp.dot(h_scr[...], w_ref[...], preferred_element_type=F32)
    for c in range(acc.shape[1] // HEAD_DIM):
        sl = slice(c * HEAD_DIM, (c + 1) * HEAD_DIM)
        y = acc[:, sl]
        if head_norm:
            y = _rms_rows(y, gain_ref[...])
        if rope:
            y = y * cos_ref[...] + pltpu.roll(y, HEAD_DIM // 2, 1) * sin_ref[...]
        y32_ref[:, sl] = y
        if emit_bf16:
            y16_ref[:, sl] = y.astype(BF16)


def _proj(x, gn, w16, layer, col0, *, gain=None, cos=None, sin=None, emit_bf16=False, tm, tn=512):
    m = x.shape[0]
    head_norm, rope = gain is not None, cos is not None
    grid = (m // tm, D_MODEL // tn)
    in_specs = [pl.BlockSpec((tm, D_MODEL), lambda i, n: (i, 0)),
                pl.BlockSpec((1, D_MODEL), lambda i, n: (0, 0)),
                pl.BlockSpec((None, D_MODEL, tn), lambda i, n: (layer, 0, col0 // tn + n))]
    args = [x, gn.reshape(1, D_MODEL), w16]
    if head_norm:
        in_specs.append(pl.BlockSpec((1, HEAD_DIM), lambda i, n: (0, 0)))
        args.append(gain.reshape(1, HEAD_DIM))
    if rope:
        nblk = cos.shape[0] // tm
        in_specs += [pl.BlockSpec((tm, HEAD_DIM), lambda i, n: (i % nblk, 0))] * 2
        args += [cos, sin]
    out_shape = [jax.ShapeDtypeStruct((m, D_MODEL), F32)]
    out_specs = [pl.BlockSpec((tm, tn), lambda i, n: (i, n))]
    if emit_bf16:
        out_shape.append(jax.ShapeDtypeStruct((m, D_MODEL), BF16))
        out_specs.append(pl.BlockSpec((tm, tn), lambda i, n: (i, n)))
    out = pl.pallas_call(
        functools.partial(_proj_kernel, head_norm=head_norm, rope=rope, emit_bf16=emit_bf16),
        grid=grid, in_specs=in_specs, out_specs=out_specs, out_shape=out_shape,
        scratch_shapes=[pltpu.VMEM((tm, D_MODEL), BF16)],
        compiler_params=_params("arbitrary", "arbitrary"), name="proj")(*args)
    return out if emit_bf16 else out[0]


def _logf_kernel(x_ref, gn_ref, w_ref, b_ref, o_ref):
    h = _rms_rows(x_ref[...], gn_ref[...]).astype(BF16)
    z = jnp.dot(h, w_ref[...], preferred_element_type=F32) + b_ref[...]
    o_ref[...] = jnp.minimum(z, 0.0) - jnp.log1p(jnp.exp(-jnp.abs(z)))


def _logf_proj(x, gn, wf16, bf, *, tm):
    m = x.shape[0]
    return pl.pallas_call(
        _logf_kernel, grid=(m // tm,),
        in_specs=[pl.BlockSpec((tm, D_MODEL), lambda i: (i, 0)),
                  pl.BlockSpec((1, D_MODEL), lambda i: (0, 0)),
                  pl.BlockSpec((D_MODEL, N_HEADS), lambda i: (0, 0)),
                  pl.BlockSpec((1, N_HEADS), lambda i: (0, 0))],
        out_specs=pl.BlockSpec((tm, N_HEADS), lambda i: (i, 0)),
        out_shape=jax.ShapeDtypeStruct((m, N_HEADS), F32),
        compiler_params=_params("arbitrary"), name="logf_proj")(
            x, gn.reshape(1, D_MODEL), wf16, bf.reshape(1, N_HEADS))


def _upper_ones(n):
    r = lax.broadcasted_iota(jnp.int32, (n, n), 0)
    c = lax.broadcasted_iota(jnp.int32, (n, n), 1)
    return (r <= c).astype(F32)


def _cumsum_rows_kernel(x_ref, o_ref, *, tk):
    u = _upper_ones(tk)
    carry = jnp.zeros((x_ref.shape[0], 1), F32)
    for s in range(x_ref.shape[1] // tk):
        c = jnp.dot(x_ref[:, s * tk:(s + 1) * tk], u, precision=HIGHEST, preferred_element_type=F32) + carry
        o_ref[:, s * tk:(s + 1) * tk] = c
        carry = c[:, tk - 1:tk]


def _cumsum_rows(x):
    return pl.pallas_call(
        functools.partial(_cumsum_rows_kernel, tk=512),
        out_shape=jax.ShapeDtypeStruct(x.shape, F32), name="cumsum_rows")(x)


def _block_mean_kernel(k_ref, o_ref):
    o_ref[0] = jnp.mean(k_ref[...], axis=0, keepdims=True)


def _block_mean(k32):
    nb = k32.shape[0] // MOBA_BLOCK
    return pl.pallas_call(
        _block_mean_kernel, grid=(nb,),
        in_specs=[pl.BlockSpec((MOBA_BLOCK, D_MODEL), lambda i: (i, 0))],
        out_specs=pl.BlockSpec((1, 1, D_MODEL), lambda i: (i, 0, 0)),
        out_shape=jax.ShapeDtypeStruct((nb, 1, D_MODEL), F32),
        compiler_params=_params("arbitrary"), name="block_mean")(k32)


def _topk_mask(gate, valid, k):
    idx = lax.broadcasted_iota(jnp.int32, gate.shape, 1)
    n = gate.shape[1]
    g = jnp.where(valid, gate, -jnp.inf)
    sel = jnp.zeros(gate.shape, F32)
    for _ in range(k):
        mx = jnp.max(g, axis=1, keepdims=True)
        is_max = (g == mx) & (g > -jnp.inf)
        first = jnp.min(jnp.where(is_max, idx, n), axis=1, keepdims=True)
        pick = idx == first
        sel = jnp.where(pick, 1.0, sel)
        g = jnp.where(pick, -jnp.inf, g)
    return sel


def _column(mat, j):
    idx = lax.broadcasted_iota(jnp.int32, mat.shape, 1)
    return jnp.sum(jnp.where(idx == j, mat, 0.0), axis=1, keepdims=True)


def _prompt_attn_kernel(q_ref, k_ref, v_ref, aux_ref, o_ref, s_scr, mx_scr, l_scr, acc_scr, *, mode, tq):
    qi = pl.program_id(2)
    tk = tq
    nbt = tq // MOBA_BLOCK
    q = q_ref[...]
    qs = (q * ATTN_SCALE).astype(BF16)
    if mode == "fox":
        row = lax.broadcasted_iota(jnp.int32, (tq, tk), 0)
        col = lax.broadcasted_iota(jnp.int32, (tq, tk), 1)
        cq_row = aux_ref[0, :, pl.ds(pl.multiple_of(qi * tq, tq), tq)]
        cq = jnp.sum(jnp.where(row == col, jnp.broadcast_to(cq_row, (tq, tk)), 0.0), axis=1, keepdims=True)
    else:
        kmean = aux_ref[0]
        gate = lax.dot_general(q, kmean, NT_DIMS, precision=HIGHEST, preferred_element_type=F32)
        blk = lax.broadcasted_iota(jnp.int32, gate.shape, 1)
        own_g = qi * nbt + lax.broadcasted_iota(jnp.int32, gate.shape, 0) // MOBA_BLOCK
        sel = _topk_mask(gate, blk < own_g, MOBA_TOPK)
        r_p = lax.broadcasted_iota(jnp.int32, (tq, MOBA_BLOCK), 0)
        c_p = lax.broadcasted_iota(jnp.int32, (tq, MOBA_BLOCK), 1)
    mx_scr[...] = jnp.full(mx_scr.shape, -jnp.inf, F32)

    def logits(j):
        k0 = pl.multiple_of(j * tk, tk)
        s = lax.dot_general(qs, k_ref[pl.ds(k0, tk), :], NT_DIMS, preferred_element_type=F32)
        if mode == "fox":
            s = s + (cq - aux_ref[0, :, pl.ds(k0, tk)])
        return s

    def put(j, s):
        s_scr[:, pl.ds(pl.multiple_of(j * tk, tk), tk)] = s
        m = mx_scr[...]
        for c in range(tk // LANES):
            m = jnp.maximum(m, s[:, c * LANES:(c + 1) * LANES])
        mx_scr[...] = m

    def past_tile(j, _):
        s = logits(j)
        if mode == "moba":
            s = jnp.concatenate(
                [jnp.where(_column(sel, j * nbt + c) > 0.5, s[:, c * MOBA_BLOCK:(c + 1) * MOBA_BLOCK], -jnp.inf)
                 for c in range(nbt)], axis=1)
        put(j, s)
        return 0

    lax.fori_loop(0, qi, past_tile, 0)
    s = logits(qi)
    if mode == "fox":
        s = jnp.where(col <= row, s, -jnp.inf)
    else:
        pieces = []
        for c in range(nbt):
            rb = r_p // MOBA_BLOCK
            own = (rb == c) & (c_p <= r_p - c * MOBA_BLOCK)
            picked = (rb > c) & (_column(sel, qi * nbt + c) > 0.5)
            pieces.append(jnp.where(own | picked, s[:, c * MOBA_BLOCK:(c + 1) * MOBA_BLOCK], -jnp.inf))
        s = jnp.concatenate(pieces, axis=1)
    put(qi, s)

    m = jnp.max(mx_scr[...], axis=1, keepdims=True)
    l_scr[...] = jnp.zeros(l_scr.shape, F32)
    acc_scr[...] = jnp.zeros(acc_scr.shape, F32)

    def pv_tile(j, _):
        k0 = pl.multiple_of(j * tk, tk)
        p = jnp.exp(s_scr[:, pl.ds(k0, tk)] - m)
        l = l_scr[...]
        for c in range(tk // LANES):
            l = l + p[:, c * LANES:(c + 1) * LANES]
        l_scr[...] = l
        acc_scr[...] += jnp.dot(p.astype(BF16), v_ref[pl.ds(k0, tk), :], preferred_element_type=F32)
        return 0

    lax.fori_loop(0, qi + 1, pv_tile, 0)
    o_ref[...] = (acc_scr[...] / jnp.sum(l_scr[...], axis=1, keepdims=True)).astype(BF16)


def _prompt_attn(q32, k16, v16, aux, *, mode, batch, seq, tq=512):
    nq = seq // tq
    if mode == "fox":
        aux_spec = pl.BlockSpec((1, 1, seq), lambda b, h, i: (b * N_HEADS + h, 0, 0))
    else:
        aux_spec = pl.BlockSpec((1, seq // MOBA_BLOCK, HEAD_DIM), lambda b, h, i: (b, 0, h))
    return pl.pallas_call(
        functools.partial(_prompt_attn_kernel, mode=mode, tq=tq),
        grid=(batch, N_HEADS, nq),
        in_specs=[pl.BlockSpec((tq, HEAD_DIM), lambda b, h, i: (b * nq + i, h)),
                  pl.BlockSpec((seq, HEAD_DIM), lambda b, h, i: (b, h)),
                  pl.BlockSpec((seq, HEAD_DIM), lambda b, h, i: (b, h)),
                  aux_spec],
        out_specs=pl.BlockSpec((tq, HEAD_DIM), lambda b, h, i: (b * nq + i, h)),
        out_shape=jax.ShapeDtypeStruct((batch * seq, D_MODEL), BF16),
        scratch_shapes=[pltpu.VMEM((tq, seq), F32),
                        pltpu.VMEM((tq, LANES), F32),
                        pltpu.VMEM((tq, LANES), F32),
                        pltpu.VMEM((tq, HEAD_DIM), F32)],
        compiler_params=_params("arbitrary", "arbitrary", "arbitrary"), name=f"prompt_attn_{mode}")(
            q32, k16, v16, aux)


def _pad_rows(x, n):
    return jnp.concatenate([x, jnp.zeros((n - x.shape[0], x.shape[1]), x.dtype)], axis=0)


def _past_cum_kernel(*refs, n_groups, ppc):
    it = iter(refs)
    next(it)
    lfn_ref = next(it)
    lf_refs = [next(it) for _ in range(ppc)]
    ck_ref, cn_ref, carry_scr = next(it), next(it), next(it)
    g = pl.program_id(1)
    upper = _upper_ones(PAGE_SIZE)
    eye = (lax.broadcasted_iota(jnp.int32, (N_HEADS, N_HEADS), 0)
           == lax.broadcasted_iota(jnp.int32, (N_HEADS, N_HEADS), 1)).astype(F32)

    def cum(lf, carry):
        lf_t = lax.dot_general(eye, lf, NT_DIMS, precision=HIGHEST, preferred_element_type=F32)
        return jnp.dot(lf_t, upper, precision=HIGHEST, preferred_element_type=F32) + carry

    @pl.when(g == 0)
    def _():
        carry_scr[...] = jnp.zeros(carry_scr.shape, F32)

    carry = carry_scr[...]
    for i in range(ppc):
        c = cum(lf_refs[i][...], carry)
        ck_ref[:, i * PAGE_SIZE:(i + 1) * PAGE_SIZE] = c
        carry = c[:, PAGE_SIZE - 1:PAGE_SIZE]
    carry_scr[...] = carry

    @pl.when(g == n_groups - 1)
    def _():
        cn_ref[...] = cum(_pad_rows(lfn_ref[0], PAGE_SIZE), carry)


def _past_cum(lf_new, cache_lf, page_table, slot, *, ppc=8):
    bs, n_pages = page_table.shape
    t_new = lf_new.shape[0] // bs
    n_groups = n_pages // ppc

    def lf_map(i):
        return lambda b, g, pt: (slot, pt[b, g * ppc + i], 0, 0)

    return pl.pallas_call(
        functools.partial(_past_cum_kernel, n_groups=n_groups, ppc=ppc),
        grid_spec=pltpu.PrefetchScalarGridSpec(
            num_scalar_prefetch=1, grid=(bs, n_groups),
            in_specs=[pl.BlockSpec((1, t_new, N_HEADS), lambda b, g, pt: (b, 0, 0))]
            + [pl.BlockSpec((None, None, PAGE_SIZE, N_HEADS), lf_map(i)) for i in range(ppc)],
            out_specs=[pl.BlockSpec((None, N_HEADS, ppc * PAGE_SIZE), lambda b, g, pt: (b, 0, g)),
                       pl.BlockSpec((None, N_HEADS, PAGE_SIZE), lambda b, g, pt: (b, 0, 0))],
            scratch_shapes=[pltpu.VMEM((N_HEADS, 1), F32)]),
        out_shape=[jax.ShapeDtypeStruct((bs, N_HEADS, n_pages * PAGE_SIZE), F32),
                   jax.ShapeDtypeStruct((bs, N_HEADS, PAGE_SIZE), F32)],
        compiler_params=_params("arbitrary", "arbitrary"), name="past_cum")(
            page_table, lf_new.reshape(bs, t_new, N_HEADS), *([cache_lf] * ppc))


def _sample_attn_kernel(*refs, mode, n_groups, t_new):
    pp = PAGES_PER_STEP
    it = iter(refs)
    next(it)
    q_ref, kn_ref, vn_ref = next(it), next(it), next(it)
    ck_ref, cn_ref = (next(it), next(it)) if mode == "fox" else (None, None)
    k_refs = [next(it) for _ in range(pp)]
    v_refs = [next(it) for _ in range(pp)]
    o_ref = next(it)
    s_scr, acc_scr, l_scr, q16_scr = next(it), next(it), next(it), next(it)
    ksum_scr = next(it) if mode == "moba" else None

    ph, g = pl.program_id(1), pl.program_id(2)
    n_rows = N_HEADS * t_new
    past = n_groups * pp * PAGE_SIZE
    n_chunks = past // LANES + 1
    sub = lax.broadcasted_iota(jnp.int32, (t_new, LANES), 0)
    lane = lax.broadcasted_iota(jnp.int32, (t_new, LANES), 1)
    pad16 = 2 * t_new

    def head_cols(h):
        return slice(h * HEAD_DIM, (h + 1) * HEAD_DIM)

    def head_rows(h):
        return slice(h * t_new, (h + 1) * t_new)

    def head_slab(h):
        return pl.ds(h, PAGE_SIZE, stride=N_HEADS)

    def page_lanes(page):
        return pl.ds(pl.multiple_of(page * PAGE_SIZE, PAGE_SIZE), PAGE_SIZE)

    @pl.when((ph == 0) & (g == 0))
    def _():
        for h in range(N_HEADS):
            q16_scr[h] = _pad_rows(q_ref[0, :, head_cols(h)] * ATTN_SCALE, pad16).astype(BF16)
        if mode == "moba":
            ksum_scr[...] = jnp.zeros(ksum_scr.shape, F32)

    @pl.when(ph == 0)
    def _():
        for i in range(pp):
            page = g * pp + i
            for h in range(N_HEADS):
                kh = k_refs[i][head_slab(h), :].astype(BF16)
                s = lax.dot_general(q16_scr[h], kh, NT_DIMS, preferred_element_type=F32)[0:t_new]
                if mode == "fox":
                    s = s - ck_ref[h:h + 1, i * PAGE_SIZE:(i + 1) * PAGE_SIZE]
                s_scr[head_rows(h), page_lanes(page)] = s
            if mode == "moba":
                k_page = k_refs[i][...].reshape(PAGE_SIZE, N_HEADS, HEAD_DIM)
                ksum_scr[page // (MOBA_BLOCK // PAGE_SIZE)] += jnp.sum(k_page, axis=0)

    @pl.when((ph == 0) & (g == n_groups - 1))
    def _():
        cqs, gates = [], []
        for h in range(N_HEADS):
            kn = _pad_rows(kn_ref[0, :, head_cols(h)], LANES).astype(BF16)
            s_new = lax.dot_general(q16_scr[h], kn, NT_DIMS, preferred_element_type=F32)[0:t_new]
            if mode == "fox":
                c_new = cn_ref[h:h + 1, :]
                cqs.append(jnp.sum(jnp.where(lane == sub, jnp.broadcast_to(c_new, (t_new, LANES)), 0.0),
                                   axis=1, keepdims=True))
                s_new = s_new - c_new
            else:
                kmean = ksum_scr[:, h, :] / MOBA_BLOCK
                gates.append(lax.dot_general(q_ref[0, :, head_cols(h)], kmean, NT_DIMS, precision=HIGHEST,
                                             preferred_element_type=F32))
            s_scr[head_rows(h), past:past + LANES] = jnp.where(lane <= sub, s_new, -jnp.inf)

        if mode == "fox":
            cq = jnp.concatenate(cqs, axis=0)
        else:
            gate = jnp.concatenate(gates, axis=0)
            sel = _topk_mask(gate, gate == gate, MOBA_TOPK)

            def mask_block(b, _):
                sl = pl.ds(pl.multiple_of(b * MOBA_BLOCK, MOBA_BLOCK), MOBA_BLOCK)
                s_scr[:, sl] = jnp.where(_column(sel, b) > 0.5, s_scr[:, sl], -jnp.inf)
                return 0
            lax.fori_loop(0, past // MOBA_BLOCK, mask_block, 0)

        def run_max(c, m):
            sl = pl.ds(pl.multiple_of(c * LANES, LANES), LANES)
            s = s_scr[:, sl]
            if mode == "fox":
                s = s + cq
                s_scr[:, sl] = s
            return jnp.maximum(m, s)
        m = lax.fori_loop(0, n_chunks, run_max, jnp.full((n_rows, LANES), -jnp.inf, F32))
        m = jnp.max(m, axis=1, keepdims=True)

        def run_exp(c, l):
            sl = pl.ds(pl.multiple_of(c * LANES, LANES), LANES)
            p = jnp.exp(s_scr[:, sl] - m)
            s_scr[:, sl] = p
            return l + p
        l = lax.fori_loop(0, n_chunks, run_exp, jnp.zeros((n_rows, LANES), F32))
        l_scr[...] = jnp.sum(l, axis=1, keepdims=True)
        acc_scr[...] = jnp.zeros(acc_scr.shape, F32)

    def pv(h, lanes, v_slab):
        p = _pad_rows(s_scr[head_rows(h), lanes], pad16).astype(BF16)
        return jnp.dot(p, v_slab.astype(BF16), preferred_element_type=F32)[0:t_new]

    @pl.when(ph == 1)
    def _():
        for i in range(pp):
            page = g * pp + i
            for h in range(N_HEADS):
                acc_scr[head_rows(h), :] += pv(h, page_lanes(page), v_refs[i][head_slab(h), :])

    @pl.when((ph == 1) & (g == n_groups - 1))
    def _():
        for h in range(N_HEADS):
            o = acc_scr[head_rows(h), :] + pv(h, slice(past, past + LANES),
                                              _pad_rows(vn_ref[0, :, head_cols(h)], LANES))
            o_ref[0, :, head_cols(h)] = (o / l_scr[head_rows(h), :]).astype(BF16)


def _sample_attn(q32, k32, v32, cum, cache_k, cache_v, page_table, slot, *, mode):
    bs, n_pages = page_table.shape
    t_new = q32.shape[0] // bs
    pp = PAGES_PER_STEP
    n_groups = n_pages // pp
    n_rows = N_HEADS * t_new
    past = n_pages * PAGE_SIZE

    def grp(ph, g, first):
        return jnp.where(ph == 0, g, n_groups - 1) if first else jnp.where(ph == 0, 0, g)

    def page_map(i, first):
        return lambda b, ph, g, pt: (slot, pt[b, grp(ph, g, first) * pp + i], 0, 0)

    cache_k = cache_k.reshape(cache_k.shape[:2] + (PAGE_SIZE * N_HEADS, HEAD_DIM))
    cache_v = cache_v.reshape(cache_v.shape[:2] + (PAGE_SIZE * N_HEADS, HEAD_DIM))
    new_spec = pl.BlockSpec((1, t_new, D_MODEL), lambda b, ph, g, pt: (b, 0, 0))
    page_block = (None, None, PAGE_SIZE * N_HEADS, HEAD_DIM)
    in_specs = [new_spec, new_spec, new_spec]
    args = [q32.reshape(bs, t_new, D_MODEL), k32.reshape(bs, t_new, D_MODEL), v32.reshape(bs, t_new, D_MODEL)]
    scratch = [pltpu.VMEM((n_rows, past + LANES), F32),
               pltpu.VMEM((n_rows, HEAD_DIM), F32),
               pltpu.VMEM((n_rows, 1), F32),
               pltpu.VMEM((N_HEADS, 2 * t_new, HEAD_DIM), BF16)]
    if mode == "fox":
        in_specs += [pl.BlockSpec((None, N_HEADS, pp * PAGE_SIZE), lambda b, ph, g, pt: (b, 0, grp(ph, g, True))),
                     pl.BlockSpec((None, N_HEADS, PAGE_SIZE), lambda b, ph, g, pt: (b, 0, 0))]
        args += list(cum)
    else:
        scratch.append(pltpu.VMEM((past // MOBA_BLOCK, N_HEADS, HEAD_DIM), F32))
    in_specs += [pl.BlockSpec(page_block, page_map(i, True)) for i in range(pp)]
    args += [cache_k] * pp
    in_specs += [pl.BlockSpec(page_block, page_map(i, False)) for i in range(pp)]
    args += [cache_v] * pp
    out = pl.pallas_call(
        functools.partial(_sample_attn_kernel, mode=mode, n_groups=n_groups, t_new=t_new),
        grid_spec=pltpu.PrefetchScalarGridSpec(
            num_scalar_prefetch=1, grid=(bs, 2, n_groups), in_specs=in_specs,
            out_specs=pl.BlockSpec((1, t_new, D_MODEL), lambda b, ph, g, pt: (b, 0, 0)),
            scratch_shapes=scratch),
        out_shape=jax.ShapeDtypeStruct((bs, t_new, D_MODEL), BF16),
        compiler_params=_params("arbitrary", "arbitrary", "arbitrary"), name=f"sample_attn_{mode}")(
            page_table, *args)
    return out.reshape(bs * t_new, D_MODEL)


def _oproj_kernel(o_ref, w_ref, x_ref, y_ref):
    y_ref[...] = x_ref[...] + jnp.dot(o_ref[...], w_ref[...], preferred_element_type=F32)


def _oproj(o16, w16, layer, x, *, tm, tn=512):
    m = x.shape[0]
    return pl.pallas_call(
        _oproj_kernel, grid=(m // tm, D_MODEL // tn),
        in_specs=[pl.BlockSpec((tm, D_MODEL), lambda i, n: (i, 0)),
                  pl.BlockSpec((None, D_MODEL, tn), lambda i, n: (layer, 0, n)),
                  pl.BlockSpec((tm, tn), lambda i, n: (i, n))],
        out_specs=pl.BlockSpec((tm, tn), lambda i, n: (i, n)),
        out_shape=jax.ShapeDtypeStruct((m, D_MODEL), F32),
        compiler_params=_params("arbitrary", "arbitrary"), name="oproj")(o16, w16, x)


def _conv_gate(gx_scr, g, cw_ref, cb_ref, g1_fix=None, g2_fix=None):
    tm = g.shape[0]
    gx_scr[HALO:HALO + tm, :] = g
    g1 = gx_scr[HALO - 1:HALO - 1 + tm, :]
    g2 = gx_scr[HALO - 2:HALO - 2 + tm, :]
    if g1_fix is not None:
        g1, g2 = g1_fix(g1), g2_fix(g2)
    acc = cb_ref[...] + g2 * cw_ref[0:1, :]
    acc = acc + g1 * cw_ref[1:2, :]
    return acc + g * cw_ref[2:3, :]


def _ffn_prompt_kernel(x_ref, xp_ref, gn_ref, wg_ref, wu_ref, cw_ref, cb_ref, wd_ref, y_ref, gt_ref,
                       h_scr, hp_scr, acc_scr, gx_scr, *, tiles_per_seq):
    i, f = pl.program_id(0), pl.program_id(1)
    tm = x_ref.shape[0]

    @pl.when(f == 0)
    def _():
        h_scr[...] = _rms_rows(x_ref[...], gn_ref[...]).astype(BF16)
        hp_scr[...] = _rms_rows(xp_ref[...], gn_ref[...]).astype(BF16)
        acc_scr[...] = jnp.zeros(acc_scr.shape, F32)

    g = jnp.dot(h_scr[...], wg_ref[...], preferred_element_type=F32)
    u = jnp.dot(h_scr[...], wu_ref[...], preferred_element_type=F32)
    g_prev = jnp.dot(hp_scr[...], wg_ref[...], preferred_element_type=F32)
    gx_scr[0:HALO, :] = jnp.where(i % tiles_per_seq == 0, 0.0, g_prev)
    a = _conv_gate(gx_scr, g, cw_ref, cb_ref)
    a = (a * jax.nn.sigmoid(a)) * u
    acc_scr[...] += jnp.dot(a.astype(BF16), wd_ref[...], preferred_element_type=F32)
    gt_ref[0] = g[tm - 8:tm, :]

    @pl.when(f == pl.num_programs(1) - 1)
    def _():
        y_ref[...] = x_ref[...] + acc_scr[...]


def _ffn_prompt(x, gn, wg16, wu16, cw, cb, wd16, layer, *, seq, tm=512, tf=512):
    m = x.shape[0]
    nf = D_FF // tf
    hb = tm // HALO
    y, gt = pl.pallas_call(
        functools.partial(_ffn_prompt_kernel, tiles_per_seq=seq // tm),
        grid=(m // tm, nf),
        in_specs=[pl.BlockSpec((tm, D_MODEL), lambda i, f: (i, 0)),
                  pl.BlockSpec((HALO, D_MODEL), lambda i, f: (jnp.maximum(i * hb - 1, 0), 0)),
                  pl.BlockSpec((1, D_MODEL), lambda i, f: (0, 0)),
                  pl.BlockSpec((None, D_MODEL, tf), lambda i, f: (layer, 0, f)),
                  pl.BlockSpec((None, D_MODEL, tf), lambda i, f: (layer, 0, f)),
                  pl.BlockSpec((None, CONV_W, tf), lambda i, f: (layer, 0, f)),
                  pl.BlockSpec((None, 1, tf), lambda i, f: (layer, 0, f)),
                  pl.BlockSpec((None, tf, D_MODEL), lambda i, f: (layer, f, 0))],
        out_specs=[pl.BlockSpec((tm, D_MODEL), lambda i, f: (i, 0)),
                   pl.BlockSpec((1, 8, tf), lambda i, f: (i, 0, f))],
        out_shape=[jax.ShapeDtypeStruct((m, D_MODEL), F32),
                   jax.ShapeDtypeStruct((m // tm, 8, D_FF), F32)],
        scratch_shapes=[pltpu.VMEM((tm, D_MODEL), BF16), pltpu.VMEM((HALO, D_MODEL), BF16),
                        pltpu.VMEM((tm, D_MODEL), F32), pltpu.VMEM((tm + HALO, tf), F32)],
        compiler_params=_params("arbitrary", "arbitrary"), name="ffn_prompt")(
            x, x, gn.reshape(1, D_MODEL), wg16, wu16, cw, cb.reshape(cb.shape[0], 1, D_FF), wd16)
    tiles = seq // tm
    tail = gt.reshape(m // seq, tiles, 8, D_FF)[:, -1, 8 - (CONV_W - 1):, :]
    return y, tail


def _ffn_sample_kernel(x_ref, gn_ref, wg_ref, wu_ref, cw_ref, cb_ref, wd_ref, s1_ref, s2_ref, y_ref, g_ref,
                       h_scr, acc_scr, gx_scr, *, t_new):
    f = pl.program_id(0)

    @pl.when(f == 0)
    def _():
        h_scr[...] = _rms_rows(x_ref[...], gn_ref[...]).astype(BF16)
        acc_scr[...] = jnp.zeros(acc_scr.shape, F32)
        gx_scr[...] = jnp.zeros(gx_scr.shape, F32)

    g = jnp.dot(h_scr[...], wg_ref[...], preferred_element_type=F32)
    u = jnp.dot(h_scr[...], wu_ref[...], preferred_element_type=F32)
    tok = lax.broadcasted_iota(jnp.int32, g.shape, 0) % t_new
    a = _conv_gate(gx_scr, g, cw_ref, cb_ref,
                   g1_fix=lambda g1: jnp.where(tok >= 1, g1, s1_ref[...]),
                   g2_fix=lambda g2: jnp.where(tok >= 2, g2, s2_ref[...]))
    a = (a * jax.nn.sigmoid(a)) * u
    acc_scr[...] += jnp.dot(a.astype(BF16), wd_ref[...], preferred_element_type=F32)
    g_ref[...] = g

    @pl.when(f == pl.num_programs(0) - 1)
    def _():
        y_ref[...] = x_ref[...] + acc_scr[...]


def _ffn_sample(x, gn, wg16, wu16, cw, cb, wd16, layer, state, *, tf=512):
    m = x.shape[0]
    bs = state.shape[0]
    t_new = m // bs
    s1 = jnp.broadcast_to(state[:, -1:, :], (bs, t_new, D_FF)).reshape(m, D_FF)
    s2 = jnp.tile(state, (1, t_new // (CONV_W - 1), 1)).reshape(m, D_FF)
    y, g = pl.pallas_call(
        functools.partial(_ffn_sample_kernel, t_new=t_new),
        grid=(D_FF // tf,),
        in_specs=[pl.BlockSpec((m, D_MODEL), lambda f: (0, 0)),
                  pl.BlockSpec((1, D_MODEL), lambda f: (0, 0)),
                  pl.BlockSpec((None, D_MODEL, tf), lambda f: (layer, 0, f)),
                  pl.BlockSpec((None, D_MODEL, tf), lambda f: (layer, 0, f)),
                  pl.BlockSpec((None, CONV_W, tf), lambda f: (layer, 0, f)),
                  pl.BlockSpec((None, 1, tf), lambda f: (layer, 0, f)),
                  pl.BlockSpec((None, tf, D_MODEL), lambda f: (layer, f, 0)),
                  pl.BlockSpec((m, tf), lambda f: (0, f)),
                  pl.BlockSpec((m, tf), lambda f: (0, f))],
        out_specs=[pl.BlockSpec((m, D_MODEL), lambda f: (0, 0)),
                   pl.BlockSpec((m, tf), lambda f: (0, f))],
        out_shape=[jax.ShapeDtypeStruct((m, D_MODEL), F32), jax.ShapeDtypeStruct((m, D_FF), F32)],
        scratch_shapes=[pltpu.VMEM((m, D_MODEL), BF16), pltpu.VMEM((m, D_MODEL), F32),
                        pltpu.VMEM((m + HALO, tf), F32)],
        compiler_params=_params("arbitrary"), name="ffn_sample")(
            x, gn.reshape(1, D_MODEL), wg16, wu16, cw, cb.reshape(cb.shape[0], 1, D_FF), wd16, s1, s2)
    return y, g.reshape(bs, t_new, D_FF)[:, t_new - (CONV_W - 1):, :]


def _rope_tables(pos):
    half = HEAD_DIM // 2
    inv_freq = ROPE_THETA ** (-jnp.arange(half, dtype=F32) / half)
    ang = pos.astype(F32)[:, None] * inv_freq[None, :]
    cos, sin = jnp.cos(ang), jnp.sin(ang)
    return jnp.concatenate([cos, cos], axis=-1), jnp.concatenate([-sin, sin], axis=-1)


def kernel(x_prompt, x_sample, cache_fox_k, cache_fox_v, cache_fox_logf, cache_moba_k, cache_moba_v,
           state_ffn_conv, page_table, attn_norm, ffn_norm, fox_w_in, fox_b_f, fox_q_norm, fox_k_norm, fox_w_o,
           moba_w_in, moba_q_norm, moba_k_norm, moba_w_o, ffn_w_gate, ffn_w_up, ffn_conv_w, ffn_conv_b,
           ffn_w_down):
    B, T, _ = x_prompt.shape
    Bs, Ts, _ = x_sample.shape
    depth = attn_norm.shape[0]
    past_len = page_table.shape[1] * PAGE_SIZE
    mp, ms = B * T, Bs * Ts
    tm_p = 1024

    fox_w16 = fox_w_in.astype(BF16)
    fox_wf16 = fox_w16[:, :, 3 * D_MODEL:]
    moba_w16 = moba_w_in.astype(BF16)
    fox_wo16, moba_wo16 = fox_w_o.astype(BF16), moba_w_o.astype(BF16)
    wg16, wu16, wd16 = ffn_w_gate.astype(BF16), ffn_w_up.astype(BF16), ffn_w_down.astype(BF16)

    cos_p, sin_p = _rope_tables(jnp.arange(T, dtype=jnp.int32))
    cos_s, sin_s = _rope_tables(past_len + jnp.arange(Ts, dtype=jnp.int32))
    cos_s, sin_s = jnp.tile(cos_s, (Bs, 1)), jnp.tile(sin_s, (Bs, 1))

    xp, xs = x_prompt.reshape(mp, D_MODEL), x_sample.reshape(ms, D_MODEL)
    outs = {k: [] for k in ("fkp", "fvp", "flp", "fks", "fvs", "fls", "mkp", "mvp", "mks", "mvs", "cvp", "cvs")}
    heads_p = (B, T, N_HEADS, HEAD_DIM)
    heads_s = (Bs, Ts, N_HEADS, HEAD_DIM)
    for i in range(depth):
        j = i // 2
        if i % 2 == 0:
            w16, gq, gk = fox_w16, fox_q_norm[j], fox_k_norm[j]
            qp = _proj(xp, attn_norm[i], w16, j, 0, gain=gq, tm=tm_p)
            kp, kp16 = _proj(xp, attn_norm[i], w16, j, D_MODEL, gain=gk, emit_bf16=True, tm=tm_p)
            vp, vp16 = _proj(xp, attn_norm[i], w16, j, 2 * D_MODEL, emit_bf16=True, tm=tm_p)
            lp = _logf_proj(xp, attn_norm[i], fox_wf16[j], fox_b_f[j], tm=tm_p)
            qs = _proj(xs, attn_norm[i], w16, j, 0, gain=gq, tm=ms)
            ks = _proj(xs, attn_norm[i], w16, j, D_MODEL, gain=gk, tm=ms)
            vs = _proj(xs, attn_norm[i], w16, j, 2 * D_MODEL, tm=ms)
            ls = _logf_proj(xs, attn_norm[i], fox_wf16[j], fox_b_f[j], tm=ms)
            lp_rows = lp.reshape(B, T, N_HEADS).transpose(0, 2, 1).reshape(B * N_HEADS, T)
            c_rows = _cumsum_rows(lp_rows).reshape(B * N_HEADS, 1, T)
            op = _prompt_attn(qp, kp16, vp16, c_rows, mode="fox", batch=B, seq=T)
            cum_s = _past_cum(ls, cache_fox_logf, page_table, j)
            os_ = _sample_attn(qs, ks, vs, cum_s, cache_fox_k, cache_fox_v, page_table, j, mode="fox")
            wo16 = fox_wo16
            outs["fkp"].append(kp.reshape(heads_p)); outs["fvp"].append(vp.reshape(heads_p))
            outs["flp"].append(lp.reshape(B, T, N_HEADS))
            outs["fks"].append(ks.reshape(heads_s)); outs["fvs"].append(vs.reshape(heads_s))
            outs["fls"].append(ls.reshape(Bs, Ts, N_HEADS))
        else:
            w16, gq, gk = moba_w16, moba_q_norm[j], moba_k_norm[j]
            qp = _proj(xp, attn_norm[i], w16, j, 0, gain=gq, cos=cos_p, sin=sin_p, tm=tm_p)
            kp, kp16 = _proj(xp, attn_norm[i], w16, j, D_MODEL, gain=gk, cos=cos_p, sin=sin_p,
                             emit_bf16=True, tm=tm_p)
            vp, vp16 = _proj(xp, attn_norm[i], w16, j, 2 * D_MODEL, emit_bf16=True, tm=tm_p)
            qs = _proj(xs, attn_norm[i], w16, j, 0, gain=gq, cos=cos_s, sin=sin_s, tm=ms)
            ks = _proj(xs, attn_norm[i], w16, j, D_MODEL, gain=gk, cos=cos_s, sin=sin_s, tm=ms)
            vs = _proj(xs, attn_norm[i], w16, j, 2 * D_MODEL, tm=ms)
            kmean = _block_mean(kp).reshape(B, T // MOBA_BLOCK, D_MODEL)
            op = _prompt_attn(qp, kp16, vp16, kmean, mode="moba", batch=B, seq=T)
            os_ = _sample_attn(qs, ks, vs, None, cache_moba_k, cache_moba_v, page_table, j, mode="moba")
            wo16 = moba_wo16
            outs["mkp"].append(kp.reshape(heads_p)); outs["mvp"].append(vp.reshape(heads_p))
            outs["mks"].append(ks.reshape(heads_s)); outs["mvs"].append(vs.reshape(heads_s))
        xp = _oproj(op, wo16, j, xp, tm=tm_p)
        xs = _oproj(os_, wo16, j, xs, tm=ms)
        xp, sp = _ffn_prompt(xp, ffn_norm[i], wg16, wu16, ffn_conv_w, ffn_conv_b, wd16, i, seq=T)
        xs, ss = _ffn_sample(xs, ffn_norm[i], wg16, wu16, ffn_conv_w, ffn_conv_b, wd16, i, state_ffn_conv[i])
        outs["cvp"].append(sp); outs["cvs"].append(ss)
    st = lambda k: jnp.stack(outs[k])
    return (xp.reshape(B, T, D_MODEL), xs.reshape(Bs, Ts, D_MODEL),
            st("fkp"), st("fvp"), st("flp"), st("mkp"), st("mvp"), st("cvp"),
            st("fks"), st("fvs"), st("fls"), st("mks"), st("mvs"), st("cvs"))
```

```python
import functools

import jax
import jax.numpy as jnp
from jax import lax
from jax.experimental import pallas as pl
from jax.experimental.pallas import tpu as pltpu

D_MODEL = 2048
N_HEADS = 16
HEAD_DIM = D_MODEL // N_HEADS
D_FF = 5632
CONV_W = 3
ROPE_THETA = 10000.0
MOBA_BLOCK = 256
MOBA_TOPK = 3
PAGE_SIZE = 128
EPS = 1e-6
ATTN_SCALE = HEAD_DIM ** -0.5

LANES = 128
HALO = 16
PAGES_PER_STEP = 4
VMEM_LIMIT = 52 * 1024 * 1024

F32 = jnp.float32
BF16 = jnp.bfloat16
HIGHEST = lax.Precision.HIGHEST
NT_DIMS = (((1,), (1,)), ((), ()))


def _params(*sem):
    return pltpu.CompilerParams(dimension_semantics=sem, vmem_limit_bytes=VMEM_LIMIT)


def _rms_rows(x, g):
    return x * lax.rsqrt(jnp.mean(x * x, axis=-1, keepdims=True) + EPS) * g


def _proj_kernel(*refs, head_norm, rope, emit_bf16):
    it = iter(refs)
    x_ref, gn_ref, w_ref = next(it), next(it), next(it)
    gain_ref = next(it) if head_norm else None
    cos_ref, sin_ref = (next(it), next(it)) if rope else (None, None)
    y32_ref = next(it)
    y16_ref = next(it) if emit_bf16 else None
    h_scr, raw_scr = next(it), next(it)
    n, last = pl.program_id(1), pl.num_programs(1) - 1

    @pl.when(n == 0)
    def _():
        h_scr[...] = _rms_rows(x_ref[...], gn_ref[...]).astype(BF16)

    def finish_previous_tile():
        for c in range(raw_scr.shape[1] // HEAD_DIM):
            sl = slice(c * HEAD_DIM, (c + 1) * HEAD_DIM)
            y = raw_scr[:, sl]
            if head_norm:
                y = _rms_rows(y, gain_ref[...])
            if rope:
                y = y * cos_ref[...] + pltpu.roll(y, HEAD_DIM // 2, 1) * sin_ref[...]
            y32_ref[:, sl] = y
            if emit_bf16:
                y16_ref[:, sl] = y.astype(BF16)

    @pl.when(n == 0)
    def _():
        raw_scr[...] = jnp.dot(h_scr[...], w_ref[...], preferred_element_type=F32)

    @pl.when((n > 0) & (n < last))
    def _():
        finish_previous_tile()
        raw_scr[...] = jnp.dot(h_scr[...], w_ref[...], preferred_element_type=F32)

    @pl.when(n == last)
    def _():
        finish_previous_tile()


def _proj(x, gn, w16, layer, col0, *, gain=None, cos=None, sin=None, emit_bf16=False, tm, tn=512):
    m = x.shape[0]
    head_norm, rope = gain is not None, cos is not None
    nn = D_MODEL // tn
    grid = (m // tm, nn + 1)
    in_specs = [pl.BlockSpec((tm, D_MODEL), lambda i, n: (i, 0)),
                pl.BlockSpec((1, D_MODEL), lambda i, n: (0, 0)),
                pl.BlockSpec((None, D_MODEL, tn), lambda i, n: (layer, 0, col0 // tn + jnp.minimum(n, nn - 1)))]
    args = [x, gn.reshape(1, D_MODEL), w16]
    if head_norm:
        in_specs.append(pl.BlockSpec((1, HEAD_DIM), lambda i, n: (0, 0)))
        args.append(gain.reshape(1, HEAD_DIM))
    if rope:
        nblk = cos.shape[0] // tm
        in_specs += [pl.BlockSpec((tm, HEAD_DIM), lambda i, n: (i % nblk, 0))] * 2
        args += [cos, sin]
    out_shape = [jax.ShapeDtypeStruct((m, D_MODEL), F32)]
    out_specs = [pl.BlockSpec((tm, tn), lambda i, n: (i, jnp.maximum(n - 1, 0)))]
    if emit_bf16:
        out_shape.append(jax.ShapeDtypeStruct((m, D_MODEL), BF16))
        out_specs.append(pl.BlockSpec((tm, tn), lambda i, n: (i, jnp.maximum(n - 1, 0))))
    out = pl.pallas_call(
        functools.partial(_proj_kernel, head_norm=head_norm, rope=rope, emit_bf16=emit_bf16),
        grid=grid, in_specs=in_specs, out_specs=out_specs, out_shape=out_shape,
        scratch_shapes=[pltpu.VMEM((tm, D_MODEL), BF16), pltpu.VMEM((tm, tn), F32)],
        compiler_params=_params("arbitrary", "arbitrary"), name="proj")(*args)
    return out if emit_bf16 else out[0]


def _logf_kernel(x_ref, gn_ref, w_ref, b_ref, o_ref):
    h = _rms_rows(x_ref[...], gn_ref[...]).astype(BF16)
    z = jnp.dot(h, w_ref[...], preferred_element_type=F32) + b_ref[...]
    o_ref[...] = jnp.minimum(z, 0.0) - jnp.log1p(jnp.exp(-jnp.abs(z)))


def _logf_proj(x, gn, wf16, bf, *, tm):
    m = x.shape[0]
    return pl.pallas_call(
        _logf_kernel, grid=(m // tm,),
        in_specs=[pl.BlockSpec((tm, D_MODEL), lambda i: (i, 0)),
                  pl.BlockSpec((1, D_MODEL), lambda i: (0, 0)),
                  pl.BlockSpec((D_MODEL, N_HEADS), lambda i: (0, 0)),
                  pl.BlockSpec((1, N_HEADS), lambda i: (0, 0))],
        out_specs=pl.BlockSpec((tm, N_HEADS), lambda i: (i, 0)),
        out_shape=jax.ShapeDtypeStruct((m, N_HEADS), F32),
        compiler_params=_params("arbitrary"), name="logf_proj")(
            x, gn.reshape(1, D_MODEL), wf16, bf.reshape(1, N_HEADS))


def _upper_ones(n):
    r = lax.broadcasted_iota(jnp.int32, (n, n), 0)
    c = lax.broadcasted_iota(jnp.int32, (n, n), 1)
    return (r <= c).astype(F32)


def _cumsum_rows_kernel(x_ref, o_ref, *, tk):
    u = _upper_ones(tk)
    carry = jnp.zeros((x_ref.shape[0], 1), F32)
    for s in range(x_ref.shape[1] // tk):
        c = jnp.dot(x_ref[:, s * tk:(s + 1) * tk], u, precision=HIGHEST, preferred_element_type=F32) + carry
        o_ref[:, s * tk:(s + 1) * tk] = c
        carry = c[:, tk - 1:tk]


def _cumsum_rows(x):
    return pl.pallas_call(
        functools.partial(_cumsum_rows_kernel, tk=512),
        out_shape=jax.ShapeDtypeStruct(x.shape, F32), name="cumsum_rows")(x)


def _block_mean_kernel(k_ref, o_ref):
    o_ref[0] = jnp.mean(k_ref[...], axis=0, keepdims=True)


def _block_mean(k32):
    nb = k32.shape[0] // MOBA_BLOCK
    return pl.pallas_call(
        _block_mean_kernel, grid=(nb,),
        in_specs=[pl.BlockSpec((MOBA_BLOCK, D_MODEL), lambda i: (i, 0))],
        out_specs=pl.BlockSpec((1, 1, D_MODEL), lambda i: (i, 0, 0)),
        out_shape=jax.ShapeDtypeStruct((nb, 1, D_MODEL), F32),
        compiler_params=_params("arbitrary"), name="block_mean")(k32)


def _topk_mask(gate, valid, k):
    idx = lax.broadcasted_iota(jnp.int32, gate.shape, 1)
    n = gate.shape[1]
    g = jnp.where(valid, gate, -jnp.inf)
    sel = jnp.zeros(gate.shape, F32)
    for _ in range(k):
        mx = jnp.max(g, axis=1, keepdims=True)
        is_max = (g == mx) & (g > -jnp.inf)
        first = jnp.min(jnp.where(is_max, idx, n), axis=1, keepdims=True)
        pick = idx == first
        sel = jnp.where(pick, 1.0, sel)
        g = jnp.where(pick, -jnp.inf, g)
    return sel


def _column(mat, j):
    idx = lax.broadcasted_iota(jnp.int32, mat.shape, 1)
    return jnp.sum(jnp.where(idx == j, mat, 0.0), axis=1, keepdims=True)


def _prompt_attn_kernel(q_ref, k_ref, v_ref, aux_ref, o_ref, s_scr, mx_scr, l_scr, acc_scr, *, mode, tq):
    qi = pl.program_id(2)
    tk = tq
    nbt = tq // MOBA_BLOCK
    q = q_ref[...]
    qs = (q * ATTN_SCALE).astype(BF16)
    if mode == "fox":
        row = lax.broadcasted_iota(jnp.int32, (tq, tk), 0)
        col = lax.broadcasted_iota(jnp.int32, (tq, tk), 1)
        cq_row = aux_ref[0, :, pl.ds(pl.multiple_of(qi * tq, tq), tq)]
        cq = jnp.sum(jnp.where(row == col, jnp.broadcast_to(cq_row, (tq, tk)), 0.0), axis=1, keepdims=True)
    else:
        kmean = aux_ref[0]
        gate = lax.dot_general(q, kmean, NT_DIMS, precision=HIGHEST, preferred_element_type=F32)
        blk = lax.broadcasted_iota(jnp.int32, gate.shape, 1)
        own_g = qi * nbt + lax.broadcasted_iota(jnp.int32, gate.shape, 0) // MOBA_BLOCK
        sel = _topk_mask(gate, blk < own_g, MOBA_TOPK)
        r_p = lax.broadcasted_iota(jnp.int32, (tq, MOBA_BLOCK), 0)
        c_p = lax.broadcasted_iota(jnp.int32, (tq, MOBA_BLOCK), 1)
    mx_scr[...] = jnp.full(mx_scr.shape, -jnp.inf, F32)

    def logits(j):
        k0 = pl.multiple_of(j * tk, tk)
        s = lax.dot_general(qs, k_ref[pl.ds(k0, tk), :], NT_DIMS, preferred_element_type=F32)
        if mode == "fox":
            s = s + (cq - aux_ref[0, :, pl.ds(k0, tk)])
        return s

    def put(js, ss):
        m = mx_scr[...]
        for j, s in zip(js, ss):
            s_scr[:, pl.ds(pl.multiple_of(j * tk, tk), tk)] = s
            for c in range(tk // LANES):
                m = jnp.maximum(m, s[:, c * LANES:(c + 1) * LANES])
        mx_scr[...] = m

    def past_tiles(js):
        ss = []
        for j in js:
            s = logits(j)
            if mode == "moba":
                s = jnp.concatenate(
                    [jnp.where(_column(sel, j * nbt + c) > 0.5, s[:, c * MOBA_BLOCK:(c + 1) * MOBA_BLOCK],
                               -jnp.inf) for c in range(nbt)], axis=1)
            ss.append(s)
        put(js, ss)

    def past_pair(jj, _):
        past_tiles([2 * jj, 2 * jj + 1])
        return 0

    lax.fori_loop(0, qi // 2, past_pair, 0)

    @pl.when(qi % 2 == 1)
    def _():
        past_tiles([qi - 1])

    s = logits(qi)
    if mode == "fox":
        s = jnp.where(col <= row, s, -jnp.inf)
    else:
        pieces = []
        for c in range(nbt):
            rb = r_p // MOBA_BLOCK
            own = (rb == c) & (c_p <= r_p - c * MOBA_BLOCK)
            picked = (rb > c) & (_column(sel, qi * nbt + c) > 0.5)
            pieces.append(jnp.where(own | picked, s[:, c * MOBA_BLOCK:(c + 1) * MOBA_BLOCK], -jnp.inf))
        s = jnp.concatenate(pieces, axis=1)
    put([qi], [s])

    m = jnp.max(mx_scr[...], axis=1, keepdims=True)
    l_scr[...] = jnp.zeros(l_scr.shape, F32)
    acc_scr[...] = jnp.zeros(acc_scr.shape, F32)

    def pv_tiles(js):
        l = l_scr[...]
        o = None
        for j in js:
            k0 = pl.multiple_of(j * tk, tk)
            p = jnp.exp(s_scr[:, pl.ds(k0, tk)] - m)
            for c in range(tk // LANES):
                l = l + p[:, c * LANES:(c + 1) * LANES]
            pv = jnp.dot(p.astype(BF16), v_ref[pl.ds(k0, tk), :], preferred_element_type=F32)
            o = pv if o is None else o + pv
        l_scr[...] = l
        acc_scr[...] += o

    def pv_pair(jj, _):
        pv_tiles([2 * jj, 2 * jj + 1])
        return 0

    lax.fori_loop(0, (qi + 1) // 2, pv_pair, 0)

    @pl.when(qi % 2 == 0)
    def _():
        pv_tiles([qi])

    o_ref[...] = (acc_scr[...] / jnp.sum(l_scr[...], axis=1, keepdims=True)).astype(BF16)


def _prompt_attn(q32, k16, v16, aux, *, mode, batch, seq, tq=512):
    nq = seq // tq
    if mode == "fox":
        aux_spec = pl.BlockSpec((1, 1, seq), lambda b, h, i: (b * N_HEADS + h, 0, 0))
    else:
        aux_spec = pl.BlockSpec((1, seq // MOBA_BLOCK, HEAD_DIM), lambda b, h, i: (b, 0, h))
    return pl.pallas_call(
        functools.partial(_prompt_attn_kernel, mode=mode, tq=tq),
        grid=(batch, N_HEADS, nq),
        in_specs=[pl.BlockSpec((tq, HEAD_DIM), lambda b, h, i: (b * nq + i, h)),
                  pl.BlockSpec((seq, HEAD_DIM), lambda b, h, i: (b, h)),
                  pl.BlockSpec((seq, HEAD_DIM), lambda b, h, i: (b, h)),
                  aux_spec],
        out_specs=pl.BlockSpec((tq, HEAD_DIM), lambda b, h, i: (b * nq + i, h)),
        out_shape=jax.ShapeDtypeStruct((batch * seq, D_MODEL), BF16),
        scratch_shapes=[pltpu.VMEM((tq, seq), F32),
                        pltpu.VMEM((tq, LANES), F32),
                        pltpu.VMEM((tq, LANES), F32),
                        pltpu.VMEM((tq, HEAD_DIM), F32)],
        compiler_params=_params("arbitrary", "arbitrary", "arbitrary"), name=f"prompt_attn_{mode}")(
            q32, k16, v16, aux)


def _pad_rows(x, n):
    return jnp.concatenate([x, jnp.zeros((n - x.shape[0], x.shape[1]), x.dtype)], axis=0)


def _past_cum_kernel(*refs, n_groups, ppc):
    it = iter(refs)
    next(it)
    lfn_ref = next(it)
    lf_refs = [next(it) for _ in range(ppc)]
    ck_ref, cn_ref, carry_scr = next(it), next(it), next(it)
    g = pl.program_id(1)
    upper = _upper_ones(PAGE_SIZE)
    eye = (lax.broadcasted_iota(jnp.int32, (N_HEADS, N_HEADS), 0)
           == lax.broadcasted_iota(jnp.int32, (N_HEADS, N_HEADS), 1)).astype(F32)

    def page_cums(lf):
        lf_t = lax.dot_general(eye, lf, NT_DIMS, precision=HIGHEST, preferred_element_type=F32)
        return [jnp.dot(lf_t[:, i * PAGE_SIZE:(i + 1) * PAGE_SIZE], upper, precision=HIGHEST,
                        preferred_element_type=F32) for i in range(lf.shape[0] // PAGE_SIZE)]

    @pl.when(g == 0)
    def _():
        carry_scr[...] = jnp.zeros(carry_scr.shape, F32)

    carry = carry_scr[...]
    for i, c in enumerate(page_cums(jnp.concatenate([r[...] for r in lf_refs], axis=0))):
        c = c + carry
        ck_ref[:, i * PAGE_SIZE:(i + 1) * PAGE_SIZE] = c
        carry = c[:, PAGE_SIZE - 1:PAGE_SIZE]
    carry_scr[...] = carry

    @pl.when(g == n_groups - 1)
    def _():
        cn_ref[...] = page_cums(_pad_rows(lfn_ref[0], PAGE_SIZE))[0] + carry


def _past_cum(lf_new, cache_lf, page_table, slot, *, ppc=8):
    bs, n_pages = page_table.shape
    t_new = lf_new.shape[0] // bs
    n_groups = n_pages // ppc

    def lf_map(i):
        return lambda b, g, pt: (slot, pt[b, g * ppc + i], 0, 0)

    return pl.pallas_call(
        functools.partial(_past_cum_kernel, n_groups=n_groups, ppc=ppc),
        grid_spec=pltpu.PrefetchScalarGridSpec(
            num_scalar_prefetch=1, grid=(bs, n_groups),
            in_specs=[pl.BlockSpec((1, t_new, N_HEADS), lambda b, g, pt: (b, 0, 0))]
            + [pl.BlockSpec((None, None, PAGE_SIZE, N_HEADS), lf_map(i)) for i in range(ppc)],
            out_specs=[pl.BlockSpec((None, N_HEADS, ppc * PAGE_SIZE), lambda b, g, pt: (b, 0, g)),
                       pl.BlockSpec((None, N_HEADS, PAGE_SIZE), lambda b, g, pt: (b, 0, 0))],
            scratch_shapes=[pltpu.VMEM((N_HEADS, 1), F32)]),
        out_shape=[jax.ShapeDtypeStruct((bs, N_HEADS, n_pages * PAGE_SIZE), F32),
                   jax.ShapeDtypeStruct((bs, N_HEADS, PAGE_SIZE), F32)],
        compiler_params=_params("arbitrary", "arbitrary"), name="past_cum")(
            page_table, lf_new.reshape(bs, t_new, N_HEADS), *([cache_lf] * ppc))


def _sample_attn_kernel(*refs, mode, n_groups, t_new):
    pp = PAGES_PER_STEP
    it = iter(refs)
    next(it)
    q_ref, kn_ref, vn_ref = next(it), next(it), next(it)
    ck_ref, cn_ref = (next(it), next(it)) if mode == "fox" else (None, None)
    k_refs = [next(it) for _ in range(pp)]
    v_refs = [next(it) for _ in range(pp)]
    o_ref = next(it)
    s_scr, acc_scr, l_scr, q16_scr = next(it), next(it), next(it), next(it)
    ksum_scr = next(it) if mode == "moba" else None

    ph, g = pl.program_id(1), pl.program_id(2)
    n_rows = N_HEADS * t_new
    past = n_groups * pp * PAGE_SIZE
    n_chunks = past // LANES + 1
    sub = lax.broadcasted_iota(jnp.int32, (t_new, LANES), 0)
    lane = lax.broadcasted_iota(jnp.int32, (t_new, LANES), 1)
    pad16 = 2 * t_new

    def head_cols(h):
        return slice(h * HEAD_DIM, (h + 1) * HEAD_DIM)

    def head_rows(h):
        return slice(h * t_new, (h + 1) * t_new)

    def head_slab(h):
        return pl.ds(h, PAGE_SIZE, stride=N_HEADS)

    def page_lanes(page):
        return pl.ds(pl.multiple_of(page * PAGE_SIZE, PAGE_SIZE), PAGE_SIZE)

    @pl.when((ph == 0) & (g == 0))
    def _():
        for h in range(N_HEADS):
            q16_scr[h] = _pad_rows(q_ref[0, :, head_cols(h)] * ATTN_SCALE, pad16).astype(BF16)
        if mode == "moba":
            ksum_scr[...] = jnp.zeros(ksum_scr.shape, F32)

    @pl.when(ph == 0)
    def _():
        for i in range(pp):
            page = g * pp + i
            for h in range(N_HEADS):
                kh = k_refs[i][head_slab(h), :].astype(BF16)
                s = lax.dot_general(q16_scr[h], kh, NT_DIMS, preferred_element_type=F32)[0:t_new]
                if mode == "fox":
                    s = s - ck_ref[h:h + 1, i * PAGE_SIZE:(i + 1) * PAGE_SIZE]
                s_scr[head_rows(h), page_lanes(page)] = s
            if mode == "moba":
                k_page = k_refs[i][...].reshape(PAGE_SIZE, N_HEADS, HEAD_DIM)
                ksum_scr[page // (MOBA_BLOCK // PAGE_SIZE)] += jnp.sum(k_page, axis=0)

    @pl.when((ph == 0) & (g == n_groups - 1))
    def _():
        cqs, gates = [], []
        for h in range(N_HEADS):
            kn = _pad_rows(kn_ref[0, :, head_cols(h)], LANES).astype(BF16)
            s_new = lax.dot_general(q16_scr[h], kn, NT_DIMS, preferred_element_type=F32)[0:t_new]
            if mode == "fox":
                c_new = cn_ref[h:h + 1, :]
                cqs.append(jnp.sum(jnp.where(lane == sub, jnp.broadcast_to(c_new, (t_new, LANES)), 0.0),
                                   axis=1, keepdims=True))
                s_new = s_new - c_new
            else:
                kmean = ksum_scr[:, h, :] / MOBA_BLOCK
                gates.append(lax.dot_general(q_ref[0, :, head_cols(h)], kmean, NT_DIMS, precision=HIGHEST,
                                             preferred_element_type=F32))
            s_scr[head_rows(h), past:past + LANES] = jnp.where(lane <= sub, s_new, -jnp.inf)

        if mode == "fox":
            cq = jnp.concatenate(cqs, axis=0)
        else:
            gate = jnp.concatenate(gates, axis=0)
            sel = _topk_mask(gate, gate == gate, MOBA_TOPK)

            def mask_block(b, _):
                sl = pl.ds(pl.multiple_of(b * MOBA_BLOCK, MOBA_BLOCK), MOBA_BLOCK)
                s_scr[:, sl] = jnp.where(_column(sel, b) > 0.5, s_scr[:, sl], -jnp.inf)
                return 0
            lax.fori_loop(0, past // MOBA_BLOCK, mask_block, 0)

        def run_max(c, m):
            sl = pl.ds(pl.multiple_of(c * LANES, LANES), LANES)
            s = s_scr[:, sl]
            if mode == "fox":
                s = s + cq
                s_scr[:, sl] = s
            return jnp.maximum(m, s)
        m = lax.fori_loop(0, n_chunks, run_max, jnp.full((n_rows, LANES), -jnp.inf, F32))
        m = jnp.max(m, axis=1, keepdims=True)

        def run_exp(c, l):
            sl = pl.ds(pl.multiple_of(c * LANES, LANES), LANES)
            p = jnp.exp(s_scr[:, sl] - m)
            s_scr[:, sl] = p
            return l + p
        l = lax.fori_loop(0, n_chunks, run_exp, jnp.zeros((n_rows, LANES), F32))
        l_scr[...] = jnp.sum(l, axis=1, keepdims=True)
        acc_scr[...] = jnp.zeros(acc_scr.shape, F32)

    def pv(h, lanes, v_slab):
        p = _pad_rows(s_scr[head_rows(h), lanes], pad16).astype(BF16)
        return jnp.dot(p, v_slab.astype(BF16), preferred_element_type=F32)[0:t_new]

    @pl.when(ph == 1)
    def _():
        for i in range(pp):
            page = g * pp + i
            for h in range(N_HEADS):
                acc_scr[head_rows(h), :] += pv(h, page_lanes(page), v_refs[i][head_slab(h), :])

    @pl.when((ph == 1) & (g == n_groups - 1))
    def _():
        for h in range(N_HEADS):
            o = acc_scr[head_rows(h), :] + pv(h, slice(past, past + LANES),
                                              _pad_rows(vn_ref[0, :, head_cols(h)], LANES))
            o_ref[0, :, head_cols(h)] = (o / l_scr[head_rows(h), :]).astype(BF16)


def _sample_attn(q32, k32, v32, cum, cache_k, cache_v, page_table, slot, *, mode):
    bs, n_pages = page_table.shape
    t_new = q32.shape[0] // bs
    pp = PAGES_PER_STEP
    n_groups = n_pages // pp
    n_rows = N_HEADS * t_new
    past = n_pages * PAGE_SIZE

    def grp(ph, g, first):
        return jnp.where(ph == 0, g, n_groups - 1) if first else jnp.where(ph == 0, 0, g)

    def page_map(i, first):
        return lambda b, ph, g, pt: (slot, pt[b, grp(ph, g, first) * pp + i], 0, 0)

    cache_k = cache_k.reshape(cache_k.shape[:2] + (PAGE_SIZE * N_HEADS, HEAD_DIM))
    cache_v = cache_v.reshape(cache_v.shape[:2] + (PAGE_SIZE * N_HEADS, HEAD_DIM))
    new_spec = pl.BlockSpec((1, t_new, D_MODEL), lambda b, ph, g, pt: (b, 0, 0))
    page_block = (None, None, PAGE_SIZE * N_HEADS, HEAD_DIM)
    in_specs = [new_spec, new_spec, new_spec]
    args = [q32.reshape(bs, t_new, D_MODEL), k32.reshape(bs, t_new, D_MODEL), v32.reshape(bs, t_new, D_MODEL)]
    scratch = [pltpu.VMEM((n_rows, past + LANES), F32),
               pltpu.VMEM((n_rows, HEAD_DIM), F32),
               pltpu.VMEM((n_rows, 1), F32),
               pltpu.VMEM((N_HEADS, 2 * t_new, HEAD_DIM), BF16)]
    if mode == "fox":
        in_specs += [pl.BlockSpec((None, N_HEADS, pp * PAGE_SIZE), lambda b, ph, g, pt: (b, 0, grp(ph, g, True))),
                     pl.BlockSpec((None, N_HEADS, PAGE_SIZE), lambda b, ph, g, pt: (b, 0, 0))]
        args += list(cum)
    else:
        scratch.append(pltpu.VMEM((past // MOBA_BLOCK, N_HEADS, HEAD_DIM), F32))
    in_specs += [pl.BlockSpec(page_block, page_map(i, True)) for i in range(pp)]
    args += [cache_k] * pp
    in_specs += [pl.BlockSpec(page_block, page_map(i, False)) for i in range(pp)]
    args += [cache_v] * pp
    out = pl.pallas_call(
        functools.partial(_sample_attn_kernel, mode=mode, n_groups=n_groups, t_new=t_new),
        grid_spec=pltpu.PrefetchScalarGridSpec(
            num_scalar_prefetch=1, grid=(bs, 2, n_groups), in_specs=in_specs,
            out_specs=pl.BlockSpec((1, t_new, D_MODEL), lambda b, ph, g, pt: (b, 0, 0)),
            scratch_shapes=scratch),
        out_shape=jax.ShapeDtypeStruct((bs, t_new, D_MODEL), BF16),
        compiler_params=_params("arbitrary", "arbitrary", "arbitrary"), name=f"sample_attn_{mode}")(
            page_table, *args)
    return out.reshape(bs * t_new, D_MODEL)


def _oproj_kernel(o_ref, w_ref, x_ref, y_ref):
    y_ref[...] = x_ref[...] + jnp.dot(o_ref[...], w_ref[...], preferred_element_type=F32)


def _oproj(o16, w16, layer, x, *, tm, tn=512):
    m = x.shape[0]
    return pl.pallas_call(
        _oproj_kernel, grid=(m // tm, D_MODEL // tn),
        in_specs=[pl.BlockSpec((tm, D_MODEL), lambda i, n: (i, 0)),
                  pl.BlockSpec((None, D_MODEL, tn), lambda i, n: (layer, 0, n)),
                  pl.BlockSpec((tm, tn), lambda i, n: (i, n))],
        out_specs=pl.BlockSpec((tm, tn), lambda i, n: (i, n)),
        out_shape=jax.ShapeDtypeStruct((m, D_MODEL), F32),
        compiler_params=_params("arbitrary", "arbitrary"), name="oproj")(o16, w16, x)


def _conv_gate(gx_scr, g, cw_ref, cb_ref, g1_fix=None, g2_fix=None):
    tm = g.shape[0]
    gx_scr[HALO:HALO + tm, :] = g
    g1 = gx_scr[HALO - 1:HALO - 1 + tm, :]
    g2 = gx_scr[HALO - 2:HALO - 2 + tm, :]
    if g1_fix is not None:
        g1, g2 = g1_fix(g1), g2_fix(g2)
    acc = cb_ref[...] + g2 * cw_ref[0:1, :]
    acc = acc + g1 * cw_ref[1:2, :]
    return acc + g * cw_ref[2:3, :]


def _ffn_prompt_kernel(x_ref, xp_ref, gn_ref, wg_ref, wu_ref, cw_ref, cb_ref, wd_ref, y_ref, gt_ref,
                       h_scr, hp_scr, acc_scr, gx_scr, a_scr, *, tiles_per_seq):
    i, f = pl.program_id(0), pl.program_id(1)
    tm = x_ref.shape[0]
    last = pl.num_programs(1) - 1

    @pl.when(f == 0)
    def _():
        h_scr[...] = _rms_rows(x_ref[...], gn_ref[...]).astype(BF16)
        hp_scr[...] = _rms_rows(xp_ref[...], gn_ref[...]).astype(BF16)
        acc_scr[...] = jnp.zeros(acc_scr.shape, F32)

    def down_previous_tile():
        acc_scr[...] += jnp.dot(a_scr[...], wd_ref[...], preferred_element_type=F32)

    def up_tile():
        g = jnp.dot(h_scr[...], wg_ref[...], preferred_element_type=F32)
        u = jnp.dot(h_scr[...], wu_ref[...], preferred_element_type=F32)
        g_prev = jnp.dot(hp_scr[...], wg_ref[...], preferred_element_type=F32)
        gx_scr[0:HALO, :] = jnp.where(i % tiles_per_seq == 0, 0.0, g_prev)
        a = _conv_gate(gx_scr, g, cw_ref, cb_ref)
        a_scr[...] = ((a * jax.nn.sigmoid(a)) * u).astype(BF16)
        gt_ref[0] = g[tm - 8:tm, :]

    @pl.when(f == 0)
    def _():
        up_tile()

    @pl.when((f > 0) & (f < last))
    def _():
        down_previous_tile()
        up_tile()

    @pl.when(f == last)
    def _():
        down_previous_tile()
        y_ref[...] = x_ref[...] + acc_scr[...]


def _ffn_prompt(x, gn, wg16, wu16, cw, cb, wd16, layer, *, seq, tm=512, tf=512):
    m = x.shape[0]
    nf = D_FF // tf
    hb = tm // HALO

    def up(f):
        return jnp.minimum(f, nf - 1)

    y, gt = pl.pallas_call(
        functools.partial(_ffn_prompt_kernel, tiles_per_seq=seq // tm),
        grid=(m // tm, nf + 1),
        in_specs=[pl.BlockSpec((tm, D_MODEL), lambda i, f: (i, 0)),
                  pl.BlockSpec((HALO, D_MODEL), lambda i, f: (jnp.maximum(i * hb - 1, 0), 0)),
                  pl.BlockSpec((1, D_MODEL), lambda i, f: (0, 0)),
                  pl.BlockSpec((None, D_MODEL, tf), lambda i, f: (layer, 0, up(f))),
                  pl.BlockSpec((None, D_MODEL, tf), lambda i, f: (layer, 0, up(f))),
                  pl.BlockSpec((None, CONV_W, tf), lambda i, f: (layer, 0, up(f))),
                  pl.BlockSpec((None, 1, tf), lambda i, f: (layer, 0, up(f))),
                  pl.BlockSpec((None, tf, D_MODEL), lambda i, f: (layer, jnp.maximum(f - 1, 0), 0))],
        out_specs=[pl.BlockSpec((tm, D_MODEL), lambda i, f: (i, 0)),
                   pl.BlockSpec((1, 8, tf), lambda i, f: (i, 0, up(f)))],
        out_shape=[jax.ShapeDtypeStruct((m, D_MODEL), F32),
                   jax.ShapeDtypeStruct((m // tm, 8, D_FF), F32)],
        scratch_shapes=[pltpu.VMEM((tm, D_MODEL), BF16), pltpu.VMEM((HALO, D_MODEL), BF16),
                        pltpu.VMEM((tm, D_MODEL), F32), pltpu.VMEM((tm + HALO, tf), F32),
                        pltpu.VMEM((tm, tf), BF16)],
        compiler_params=_params("arbitrary", "arbitrary"), name="ffn_prompt")(
            x, x, gn.reshape(1, D_MODEL), wg16, wu16, cw, cb.reshape(cb.shape[0], 1, D_FF), wd16)
    tiles = seq // tm
    tail = gt.reshape(m // seq, tiles, 8, D_FF)[:, -1, 8 - (CONV_W - 1):, :]
    return y, tail


def _ffn_sample_kernel(x_ref, gn_ref, wg_ref, wu_ref, cw_ref, cb_ref, wd_ref, s1_ref, s2_ref, y_ref, g_ref,
                       h_scr, acc_scr, gx_scr, *, t_new):
    f = pl.program_id(0)

    @pl.when(f == 0)
    def _():
        h_scr[...] = _rms_rows(x_ref[...], gn_ref[...]).astype(BF16)
        acc_scr[...] = jnp.zeros(acc_scr.shape, F32)
        gx_scr[...] = jnp.zeros(gx_scr.shape, F32)

    g = jnp.dot(h_scr[...], wg_ref[...], preferred_element_type=F32)
    u = jnp.dot(h_scr[...], wu_ref[...], preferred_element_type=F32)
    tok = lax.broadcasted_iota(jnp.int32, g.shape, 0) % t_new
    a = _conv_gate(gx_scr, g, cw_ref, cb_ref,
                   g1_fix=lambda g1: jnp.where(tok >= 1, g1, s1_ref[...]),
                   g2_fix=lambda g2: jnp.where(tok >= 2, g2, s2_ref[...]))
    a = (a * jax.nn.sigmoid(a)) * u
    acc_scr[...] += jnp.dot(a.astype(BF16), wd_ref[...], preferred_element_type=F32)
    g_ref[...] = g

    @pl.when(f == pl.num_programs(0) - 1)
    def _():
        y_ref[...] = x_ref[...] + acc_scr[...]


def _ffn_sample(x, gn, wg16, wu16, cw, cb, wd16, layer, state, *, tf=512):
    m = x.shape[0]
    bs = state.shape[0]
    t_new = m // bs
    s1 = jnp.broadcast_to(state[:, -1:, :], (bs, t_new, D_FF)).reshape(m, D_FF)
    s2 = jnp.tile(state, (1, t_new // (CONV_W - 1), 1)).reshape(m, D_FF)
    y, g = pl.pallas_call(
        functools.partial(_ffn_sample_kernel, t_new=t_new),
        grid=(D_FF // tf,),
        in_specs=[pl.BlockSpec((m, D_MODEL), lambda f: (0, 0)),
                  pl.BlockSpec((1, D_MODEL), lambda f: (0, 0)),
                  pl.BlockSpec((None, D_MODEL, tf), lambda f: (layer, 0, f)),
                  pl.BlockSpec((None, D_MODEL, tf), lambda f: (layer, 0, f)),
                  pl.BlockSpec((None, CONV_W, tf), lambda f: (layer, 0, f)),
                  pl.BlockSpec((None, 1, tf), lambda f: (layer, 0, f)),
                  pl.BlockSpec((None, tf, D_MODEL), lambda f: (layer, f, 0)),
                  pl.BlockSpec((m, tf), lambda f: (0, f)),
                  pl.BlockSpec((m, tf), lambda f: (0, f))],
        out_specs=[pl.BlockSpec((m, D_MODEL), lambda f: (0, 0)),
                   pl.BlockSpec((m, tf), lambda f: (0, f))],
        out_shape=[jax.ShapeDtypeStruct((m, D_MODEL), F32), jax.ShapeDtypeStruct((m, D_FF), F32)],
        scratch_shapes=[pltpu.VMEM((m, D_MODEL), BF16), pltpu.VMEM((m, D_MODEL), F32),
                        pltpu.VMEM((m + HALO, tf), F32)],
        compiler_params=_params("arbitrary"), name="ffn_sample")(
            x, gn.reshape(1, D_MODEL), wg16, wu16, cw, cb.reshape(cb.shape[0], 1, D_FF), wd16, s1, s2)
    return y, g.reshape(bs, t_new, D_FF)[:, t_new - (CONV_W - 1):, :]


def _rope_tables(pos):
    half = HEAD_DIM // 2
    inv_freq = ROPE_THETA ** (-jnp.arange(half, dtype=F32) / half)
    ang = pos.astype(F32)[:, None] * inv_freq[None, :]
    cos, sin = jnp.cos(ang), jnp.sin(ang)
    return jnp.concatenate([cos, cos], axis=-1), jnp.concatenate([-sin, sin], axis=-1)


def kernel(x_prompt, x_sample, cache_fox_k, cache_fox_v, cache_fox_logf, cache_moba_k, cache_moba_v,
           state_ffn_conv, page_table, attn_norm, ffn_norm, fox_w_in, fox_b_f, fox_q_norm, fox_k_norm, fox_w_o,
           moba_w_in, moba_q_norm, moba_k_norm, moba_w_o, ffn_w_gate, ffn_w_up, ffn_conv_w, ffn_conv_b,
           ffn_w_down):
    B, T, _ = x_prompt.shape
    Bs, Ts, _ = x_sample.shape
    depth = attn_norm.shape[0]
    past_len = page_table.shape[1] * PAGE_SIZE
    mp, ms = B * T, Bs * Ts
    tm_p = 1024

    fox_w16 = fox_w_in.astype(BF16)
    fox_wf16 = fox_w16[:, :, 3 * D_MODEL:]
    moba_w16 = moba_w_in.astype(BF16)
    fox_wo16, moba_wo16 = fox_w_o.astype(BF16), moba_w_o.astype(BF16)
    wg16, wu16, wd16 = ffn_w_gate.astype(BF16), ffn_w_up.astype(BF16), ffn_w_down.astype(BF16)

    cos_p, sin_p = _rope_tables(jnp.arange(T, dtype=jnp.int32))
    cos_s, sin_s = _rope_tables(past_len + jnp.arange(Ts, dtype=jnp.int32))
    cos_s, sin_s = jnp.tile(cos_s, (Bs, 1)), jnp.tile(sin_s, (Bs, 1))

    xp, xs = x_prompt.reshape(mp, D_MODEL), x_sample.reshape(ms, D_MODEL)
    outs = {k: [] for k in ("fkp", "fvp", "flp", "fks", "fvs", "fls", "mkp", "mvp", "mks", "mvs", "cvp", "cvs")}
    heads_p = (B, T, N_HEADS, HEAD_DIM)
    heads_s = (Bs, Ts, N_HEADS, HEAD_DIM)
    for i in range(depth):
        j = i // 2
        if i % 2 == 0:
            w16, gq, gk = fox_w16, fox_q_norm[j], fox_k_norm[j]
            qp = _proj(xp, attn_norm[i], w16, j, 0, gain=gq, tm=tm_p)
            kp, kp16 = _proj(xp, attn_norm[i], w16, j, D_MODEL, gain=gk, emit_bf16=True, tm=tm_p)
            vp, vp16 = _proj(xp, attn_norm[i], w16, j, 2 * D_MODEL, emit_bf16=True, tm=tm_p)
            lp = _logf_proj(xp, attn_norm[i], fox_wf16[j], fox_b_f[j], tm=tm_p)
            qs = _proj(xs, attn_norm[i], w16, j, 0, gain=gq, tm=ms)
            ks = _proj(xs, attn_norm[i], w16, j, D_MODEL, gain=gk, tm=ms)
            vs = _proj(xs, attn_norm[i], w16, j, 2 * D_MODEL, tm=ms)
            ls = _logf_proj(xs, attn_norm[i], fox_wf16[j], fox_b_f[j], tm=ms)
            lp_rows = lp.reshape(B, T, N_HEADS).transpose(0, 2, 1).reshape(B * N_HEADS, T)
            c_rows = _cumsum_rows(lp_rows).reshape(B * N_HEADS, 1, T)
            op = _prompt_attn(qp, kp16, vp16, c_rows, mode="fox", batch=B, seq=T)
            cum_s = _past_cum(ls, cache_fox_logf, page_table, j)
            os_ = _sample_attn(qs, ks, vs, cum_s, cache_fox_k, cache_fox_v, page_table, j, mode="fox")
            wo16 = fox_wo16
            outs["fkp"].append(kp.reshape(heads_p)); outs["fvp"].append(vp.reshape(heads_p))
            outs["flp"].append(lp.reshape(B, T, N_HEADS))
            outs["fks"].append(ks.reshape(heads_s)); outs["fvs"].append(vs.reshape(heads_s))
            outs["fls"].append(ls.reshape(Bs, Ts, N_HEADS))
        else:
            w16, gq, gk = moba_w16, moba_q_norm[j], moba_k_norm[j]
            qp = _proj(xp, attn_norm[i], w16, j, 0, gain=gq, cos=cos_p, sin=sin_p, tm=tm_p)
            kp, kp16 = _proj(xp, attn_norm[i], w16, j, D_MODEL, gain=gk, cos=cos_p, sin=sin_p,
                             emit_bf16=True, tm=tm_p)
            vp, vp16 = _proj(xp, attn_norm[i], w16, j, 2 * D_MODEL, emit_bf16=True, tm=tm_p)
            qs = _proj(xs, attn_norm[i], w16, j, 0, gain=gq, cos=cos_s, sin=sin_s, tm=ms)
            ks = _proj(xs, attn_norm[i], w16, j, D_MODEL, gain=gk, cos=cos_s, sin=sin_s, tm=ms)
            vs = _proj(xs, attn_norm[i], w16, j, 2 * D_MODEL, tm=ms)
            kmean = _block_mean(kp).reshape(B, T // MOBA_BLOCK, D_MODEL)
            op = _prompt_attn(qp, kp16, vp16, kmean, mode="moba", batch=B, seq=T)
            os_ = _sample_attn(qs, ks, vs, None, cache_moba_k, cache_moba_v, page_table, j, mode="moba")
            wo16 = moba_wo16
            outs["mkp"].append(kp.reshape(heads_p)); outs["mvp"].append(vp.reshape(heads_p))
            outs["mks"].append(ks.reshape(heads_s)); outs["mvs"].append(vs.reshape(heads_s))
        xp = _oproj(op, wo16, j, xp, tm=tm_p)
        xs = _oproj(os_, wo16, j, xs, tm=ms)
        xp, sp = _ffn_prompt(xp, ffn_norm[i], wg16, wu16, ffn_conv_w, ffn_conv_b, wd16, i, seq=T)
        xs, ss = _ffn_sample(xs, ffn_norm[i], wg16, wu16, ffn_conv_w, ffn_conv_b, wd16, i, state_ffn_conv[i])
        outs["cvp"].append(sp); outs["cvs"].append(ss)
    st = lambda k: jnp.stack(outs[k])
    return (xp.reshape(B, T, D_MODEL), xs.reshape(Bs, Ts, D_MODEL),
            st("fkp"), st("fvp"), st("flp"), st("mkp"), st("mvp"), st("cvp"),
            st("fks"), st("fvs"), st("fls"), st("mks"), st("mvs"), st("cvs"))
```

```python
import functools

import jax
import jax.numpy as jnp
from jax import lax
from jax.experimental import pallas as pl
from jax.experimental.pallas import tpu as pltpu

D_MODEL = 2048
N_HEADS = 16
HEAD_DIM = D_MODEL // N_HEADS
D_FF = 5632
CONV_W = 3
ROPE_THETA = 10000.0
MOBA_BLOCK = 256
MOBA_TOPK = 3
PAGE_SIZE = 128
EPS = 1e-6
ATTN_SCALE = HEAD_DIM ** -0.5

LANES = 128
HALO = 16
PAGES_PER_STEP = 4
HEAD_GROUP = 8
HEAD_GROUPS = N_HEADS // HEAD_GROUP
VMEM_LIMIT = 52 * 1024 * 1024

F32 = jnp.float32
BF16 = jnp.bfloat16
HIGHEST = lax.Precision.HIGHEST
NT_DIMS = (((1,), (1,)), ((), ()))


def _params(*sem):
    return pltpu.CompilerParams(dimension_semantics=sem, vmem_limit_bytes=VMEM_LIMIT)


def _rms_rows(x, g):
    return x * lax.rsqrt(jnp.mean(x * x, axis=-1, keepdims=True) + EPS) * g


def _proj_kernel(*refs, head_norm, rope, emit_bf16):
    it = iter(refs)
    x_ref, gn_ref, w_ref = next(it), next(it), next(it)
    gain_ref = next(it) if head_norm else None
    cos_ref, sin_ref = (next(it), next(it)) if rope else (None, None)
    y32_ref = next(it)
    y16_ref = next(it) if emit_bf16 else None
    h_scr, raw_scr = next(it), next(it)
    n, last = pl.program_id(1), pl.num_programs(1) - 1

    @pl.when(n == 0)
    def _():
        h_scr[...] = _rms_rows(x_ref[...], gn_ref[...]).astype(BF16)

    def finish_previous_tile():
        for c in range(raw_scr.shape[1] // HEAD_DIM):
            sl = slice(c * HEAD_DIM, (c + 1) * HEAD_DIM)
            y = raw_scr[:, sl]
            if head_norm:
                y = _rms_rows(y, gain_ref[...])
            if rope:
                y = y * cos_ref[...] + pltpu.roll(y, HEAD_DIM // 2, 1) * sin_ref[...]
            y32_ref[:, sl] = y
            if emit_bf16:
                y16_ref[:, sl] = y.astype(BF16)

    @pl.when(n == 0)
    def _():
        raw_scr[...] = jnp.dot(h_scr[...], w_ref[...], preferred_element_type=F32)

    @pl.when((n > 0) & (n < last))
    def _():
        finish_previous_tile()
        raw_scr[...] = jnp.dot(h_scr[...], w_ref[...], preferred_element_type=F32)

    @pl.when(n == last)
    def _():
        finish_previous_tile()


def _proj(x, gn, w16, layer, col0, *, gain=None, cos=None, sin=None, emit_bf16=False, tm, tn=512):
    m = x.shape[0]
    head_norm, rope = gain is not None, cos is not None
    nn = D_MODEL // tn
    grid = (m // tm, nn + 1)
    in_specs = [pl.BlockSpec((tm, D_MODEL), lambda i, n: (i, 0)),
                pl.BlockSpec((1, D_MODEL), lambda i, n: (0, 0)),
                pl.BlockSpec((None, D_MODEL, tn), lambda i, n: (layer, 0, col0 // tn + jnp.minimum(n, nn - 1)))]
    args = [x, gn.reshape(1, D_MODEL), w16]
    if head_norm:
        in_specs.append(pl.BlockSpec((1, HEAD_DIM), lambda i, n: (0, 0)))
        args.append(gain.reshape(1, HEAD_DIM))
    if rope:
        nblk = cos.shape[0] // tm
        in_specs += [pl.BlockSpec((tm, HEAD_DIM), lambda i, n: (i % nblk, 0))] * 2
        args += [cos, sin]
    out_shape = [jax.ShapeDtypeStruct((m, D_MODEL), F32)]
    out_specs = [pl.BlockSpec((tm, tn), lambda i, n: (i, jnp.maximum(n - 1, 0)))]
    if emit_bf16:
        out_shape.append(jax.ShapeDtypeStruct((m, D_MODEL), BF16))
        out_specs.append(pl.BlockSpec((tm, tn), lambda i, n: (i, jnp.maximum(n - 1, 0))))
    out = pl.pallas_call(
        functools.partial(_proj_kernel, head_norm=head_norm, rope=rope, emit_bf16=emit_bf16),
        grid=grid, in_specs=in_specs, out_specs=out_specs, out_shape=out_shape,
        scratch_shapes=[pltpu.VMEM((tm, D_MODEL), BF16), pltpu.VMEM((tm, tn), F32)],
        compiler_params=_params("arbitrary", "arbitrary"), name="proj")(*args)
    return out if emit_bf16 else out[0]


def _logf_kernel(x_ref, gn_ref, w_ref, b_ref, o_ref):
    h = _rms_rows(x_ref[...], gn_ref[...]).astype(BF16)
    z = jnp.dot(h, w_ref[...], preferred_element_type=F32) + b_ref[...]
    o_ref[...] = jnp.minimum(z, 0.0) - jnp.log1p(jnp.exp(-jnp.abs(z)))


def _logf_proj(x, gn, wf16, bf, *, tm):
    m = x.shape[0]
    return pl.pallas_call(
        _logf_kernel, grid=(m // tm,),
        in_specs=[pl.BlockSpec((tm, D_MODEL), lambda i: (i, 0)),
                  pl.BlockSpec((1, D_MODEL), lambda i: (0, 0)),
                  pl.BlockSpec((D_MODEL, N_HEADS), lambda i: (0, 0)),
                  pl.BlockSpec((1, N_HEADS), lambda i: (0, 0))],
        out_specs=pl.BlockSpec((tm, N_HEADS), lambda i: (i, 0)),
        out_shape=jax.ShapeDtypeStruct((m, N_HEADS), F32),
        compiler_params=_params("arbitrary"), name="logf_proj")(
            x, gn.reshape(1, D_MODEL), wf16, bf.reshape(1, N_HEADS))


def _upper_ones(n):
    r = lax.broadcasted_iota(jnp.int32, (n, n), 0)
    c = lax.broadcasted_iota(jnp.int32, (n, n), 1)
    return (r <= c).astype(F32)


def _cumsum_rows_kernel(x_ref, o_ref, *, tk):
    u = _upper_ones(tk)
    carry = jnp.zeros((x_ref.shape[0], 1), F32)
    for s in range(x_ref.shape[1] // tk):
        c = jnp.dot(x_ref[:, s * tk:(s + 1) * tk], u, precision=HIGHEST, preferred_element_type=F32) + carry
        o_ref[:, s * tk:(s + 1) * tk] = c
        carry = c[:, tk - 1:tk]


def _cumsum_rows(x):
    return pl.pallas_call(
        functools.partial(_cumsum_rows_kernel, tk=512),
        out_shape=jax.ShapeDtypeStruct(x.shape, F32), name="cumsum_rows")(x)


def _block_mean_kernel(k_ref, o_ref):
    o_ref[0] = jnp.mean(k_ref[...], axis=0, keepdims=True)


def _block_mean(k32):
    nb = k32.shape[0] // MOBA_BLOCK
    return pl.pallas_call(
        _block_mean_kernel, grid=(nb,),
        in_specs=[pl.BlockSpec((MOBA_BLOCK, D_MODEL), lambda i: (i, 0))],
        out_specs=pl.BlockSpec((1, 1, D_MODEL), lambda i: (i, 0, 0)),
        out_shape=jax.ShapeDtypeStruct((nb, 1, D_MODEL), F32),
        compiler_params=_params("arbitrary"), name="block_mean")(k32)


def _topk_mask(gate, valid, k):
    idx = lax.broadcasted_iota(jnp.int32, gate.shape, 1)
    n = gate.shape[1]
    g = jnp.where(valid, gate, -jnp.inf)
    sel = jnp.zeros(gate.shape, F32)
    for _ in range(k):
        mx = jnp.max(g, axis=1, keepdims=True)
        is_max = (g == mx) & (g > -jnp.inf)
        first = jnp.min(jnp.where(is_max, idx, n), axis=1, keepdims=True)
        pick = idx == first
        sel = jnp.where(pick, 1.0, sel)
        g = jnp.where(pick, -jnp.inf, g)
    return sel


def _column(mat, j):
    idx = lax.broadcasted_iota(jnp.int32, mat.shape, 1)
    return jnp.sum(jnp.where(idx == j, mat, 0.0), axis=1, keepdims=True)


def _prompt_attn_kernel(q_ref, k_ref, v_ref, aux_ref, o_ref, s_scr, mx_scr, l_scr, acc_scr, *, mode, tq):
    qi = pl.program_id(2)
    tk = tq
    nbt = tq // MOBA_BLOCK
    q = q_ref[...]
    qs = (q * ATTN_SCALE).astype(BF16)
    if mode == "fox":
        row = lax.broadcasted_iota(jnp.int32, (tq, tk), 0)
        col = lax.broadcasted_iota(jnp.int32, (tq, tk), 1)
        cq_row = aux_ref[0, :, pl.ds(pl.multiple_of(qi * tq, tq), tq)]
        cq = jnp.sum(jnp.where(row == col, jnp.broadcast_to(cq_row, (tq, tk)), 0.0), axis=1, keepdims=True)
    else:
        kmean = aux_ref[0]
        gate = lax.dot_general(q, kmean, NT_DIMS, precision=HIGHEST, preferred_element_type=F32)
        blk = lax.broadcasted_iota(jnp.int32, gate.shape, 1)
        own_g = qi * nbt + lax.broadcasted_iota(jnp.int32, gate.shape, 0) // MOBA_BLOCK
        sel = _topk_mask(gate, blk < own_g, MOBA_TOPK)
        r_p = lax.broadcasted_iota(jnp.int32, (tq, MOBA_BLOCK), 0)
        c_p = lax.broadcasted_iota(jnp.int32, (tq, MOBA_BLOCK), 1)
    mx_scr[...] = jnp.full(mx_scr.shape, -jnp.inf, F32)

    def logits(j):
        k0 = pl.multiple_of(j * tk, tk)
        s = lax.dot_general(qs, k_ref[pl.ds(k0, tk), :], NT_DIMS, preferred_element_type=F32)
        if mode == "fox":
            s = s + (cq - aux_ref[0, :, pl.ds(k0, tk)])
        return s

    def put(js, ss):
        m = mx_scr[...]
        for j, s in zip(js, ss):
            s_scr[:, pl.ds(pl.multiple_of(j * tk, tk), tk)] = s
            for c in range(tk // LANES):
                m = jnp.maximum(m, s[:, c * LANES:(c + 1) * LANES])
        mx_scr[...] = m

    def past_tiles(js):
        ss = []
        for j in js:
            s = logits(j)
            if mode == "moba":
                s = jnp.concatenate(
                    [jnp.where(_column(sel, j * nbt + c) > 0.5, s[:, c * MOBA_BLOCK:(c + 1) * MOBA_BLOCK],
                               -jnp.inf) for c in range(nbt)], axis=1)
            ss.append(s)
        put(js, ss)

    def past_pair(jj, _):
        past_tiles([2 * jj, 2 * jj + 1])
        return 0

    lax.fori_loop(0, qi // 2, past_pair, 0)

    @pl.when(qi % 2 == 1)
    def _():
        past_tiles([qi - 1])

    s = logits(qi)
    if mode == "fox":
        s = jnp.where(col <= row, s, -jnp.inf)
    else:
        pieces = []
        for c in range(nbt):
            rb = r_p // MOBA_BLOCK
            own = (rb == c) & (c_p <= r_p - c * MOBA_BLOCK)
            picked = (rb > c) & (_column(sel, qi * nbt + c) > 0.5)
            pieces.append(jnp.where(own | picked, s[:, c * MOBA_BLOCK:(c + 1) * MOBA_BLOCK], -jnp.inf))
        s = jnp.concatenate(pieces, axis=1)
    put([qi], [s])

    m = jnp.max(mx_scr[...], axis=1, keepdims=True)
    l_scr[...] = jnp.zeros(l_scr.shape, F32)
    acc_scr[...] = jnp.zeros(acc_scr.shape, F32)

    def pv_tiles(js):
        l = l_scr[...]
        o = None
        for j in js:
            k0 = pl.multiple_of(j * tk, tk)
            p = jnp.exp(s_scr[:, pl.ds(k0, tk)] - m)
            for c in range(tk // LANES):
                l = l + p[:, c * LANES:(c + 1) * LANES]
            pv = jnp.dot(p.astype(BF16), v_ref[pl.ds(k0, tk), :], preferred_element_type=F32)
            o = pv if o is None else o + pv
        l_scr[...] = l
        acc_scr[...] += o

    def pv_pair(jj, _):
        pv_tiles([2 * jj, 2 * jj + 1])
        return 0

    lax.fori_loop(0, (qi + 1) // 2, pv_pair, 0)

    @pl.when(qi % 2 == 0)
    def _():
        pv_tiles([qi])

    o_ref[...] = (acc_scr[...] / jnp.sum(l_scr[...], axis=1, keepdims=True)).astype(BF16)


def _prompt_attn(q32, k16, v16, aux, *, mode, batch, seq, tq=512):
    nq = seq // tq
    if mode == "fox":
        aux_spec = pl.BlockSpec((1, 1, seq), lambda b, h, i: (b * N_HEADS + h, 0, 0))
    else:
        aux_spec = pl.BlockSpec((1, seq // MOBA_BLOCK, HEAD_DIM), lambda b, h, i: (b, 0, h))
    return pl.pallas_call(
        functools.partial(_prompt_attn_kernel, mode=mode, tq=tq),
        grid=(batch, N_HEADS, nq),
        in_specs=[pl.BlockSpec((tq, HEAD_DIM), lambda b, h, i: (b * nq + i, h)),
                  pl.BlockSpec((seq, HEAD_DIM), lambda b, h, i: (b, h)),
                  pl.BlockSpec((seq, HEAD_DIM), lambda b, h, i: (b, h)),
                  aux_spec],
        out_specs=pl.BlockSpec((tq, HEAD_DIM), lambda b, h, i: (b * nq + i, h)),
        out_shape=jax.ShapeDtypeStruct((batch * seq, D_MODEL), BF16),
        scratch_shapes=[pltpu.VMEM((tq, seq), F32),
                        pltpu.VMEM((tq, LANES), F32),
                        pltpu.VMEM((tq, LANES), F32),
                        pltpu.VMEM((tq, HEAD_DIM), F32)],
        compiler_params=_params("arbitrary", "arbitrary", "arbitrary"), name=f"prompt_attn_{mode}")(
            q32, k16, v16, aux)


def _pad_rows(x, n):
    return jnp.concatenate([x, jnp.zeros((n - x.shape[0], x.shape[1]), x.dtype)], axis=0)


def _head_slab(page_refs, h):
    ref = page_refs[h // HEAD_GROUP].reshape(PAGE_SIZE * HEAD_GROUP, HEAD_DIM)
    return ref[pl.ds(h % HEAD_GROUP, PAGE_SIZE, stride=HEAD_GROUP), :]


def _past_cum_kernel(*refs, n_groups, ppc):
    it = iter(refs)
    next(it)
    lfn_ref = next(it)
    lf_refs = [next(it) for _ in range(ppc)]
    ck_ref, cn_ref, carry_scr = next(it), next(it), next(it)
    g = pl.program_id(1)
    upper = _upper_ones(PAGE_SIZE)
    eye = (lax.broadcasted_iota(jnp.int32, (N_HEADS, N_HEADS), 0)
           == lax.broadcasted_iota(jnp.int32, (N_HEADS, N_HEADS), 1)).astype(F32)

    def page_cums(lf):
        lf_t = lax.dot_general(eye, lf, NT_DIMS, precision=HIGHEST, preferred_element_type=F32)
        return [jnp.dot(lf_t[:, i * PAGE_SIZE:(i + 1) * PAGE_SIZE], upper, precision=HIGHEST,
                        preferred_element_type=F32) for i in range(lf.shape[0] // PAGE_SIZE)]

    @pl.when(g == 0)
    def _():
        carry_scr[...] = jnp.zeros(carry_scr.shape, F32)

    carry = carry_scr[...]
    for i, c in enumerate(page_cums(jnp.concatenate([r[...] for r in lf_refs], axis=0))):
        c = c + carry
        ck_ref[:, i * PAGE_SIZE:(i + 1) * PAGE_SIZE] = c
        carry = c[:, PAGE_SIZE - 1:PAGE_SIZE]
    carry_scr[...] = carry

    @pl.when(g == n_groups - 1)
    def _():
        cn_ref[...] = page_cums(_pad_rows(lfn_ref[0], PAGE_SIZE))[0] + carry


def _past_cum(lf_new, cache_lf, page_table, slot, *, ppc=8):
    bs, n_pages = page_table.shape
    t_new = lf_new.shape[0] // bs
    n_groups = n_pages // ppc

    def lf_map(i):
        return lambda b, g, pt: (slot, pt[b, g * ppc + i], 0, 0)

    return pl.pallas_call(
        functools.partial(_past_cum_kernel, n_groups=n_groups, ppc=ppc),
        grid_spec=pltpu.PrefetchScalarGridSpec(
            num_scalar_prefetch=1, grid=(bs, n_groups),
            in_specs=[pl.BlockSpec((1, t_new, N_HEADS), lambda b, g, pt: (b, 0, 0))]
            + [pl.BlockSpec((None, None, PAGE_SIZE, N_HEADS), lf_map(i)) for i in range(ppc)],
            out_specs=[pl.BlockSpec((None, N_HEADS, ppc * PAGE_SIZE), lambda b, g, pt: (b, 0, g)),
                       pl.BlockSpec((None, N_HEADS, PAGE_SIZE), lambda b, g, pt: (b, 0, 0))],
            scratch_shapes=[pltpu.VMEM((N_HEADS, 1), F32)]),
        out_shape=[jax.ShapeDtypeStruct((bs, N_HEADS, n_pages * PAGE_SIZE), F32),
                   jax.ShapeDtypeStruct((bs, N_HEADS, PAGE_SIZE), F32)],
        compiler_params=_params("arbitrary", "arbitrary"), name="past_cum")(
            page_table, lf_new.reshape(bs, t_new, N_HEADS), *([cache_lf] * ppc))


def _sample_attn_kernel(*refs, mode, n_groups, t_new):
    pp = PAGES_PER_STEP
    it = iter(refs)
    next(it)
    q_ref, kn_ref, vn_ref = next(it), next(it), next(it)
    ck_ref, cn_ref = (next(it), next(it)) if mode == "fox" else (None, None)
    k_refs = [[next(it) for _ in range(HEAD_GROUPS)] for _ in range(pp)]
    v_refs = [[next(it) for _ in range(HEAD_GROUPS)] for _ in range(pp)]
    o_ref = next(it)
    s_scr, acc_scr, l_scr, q16_scr = next(it), next(it), next(it), next(it)
    ksum_scr = next(it) if mode == "moba" else None

    ph, g = pl.program_id(1), pl.program_id(2)
    n_rows = N_HEADS * t_new
    past = n_groups * pp * PAGE_SIZE
    n_chunks = past // LANES + 1
    sub = lax.broadcasted_iota(jnp.int32, (t_new, LANES), 0)
    lane = lax.broadcasted_iota(jnp.int32, (t_new, LANES), 1)
    pad16 = 2 * t_new

    def head_cols(h):
        return slice(h * HEAD_DIM, (h + 1) * HEAD_DIM)

    def head_rows(h):
        return slice(h * t_new, (h + 1) * t_new)

    def page_lanes(page):
        return pl.ds(pl.multiple_of(page * PAGE_SIZE, PAGE_SIZE), PAGE_SIZE)

    @pl.when((ph == 0) & (g == 0))
    def _():
        for h in range(N_HEADS):
            q16_scr[h] = _pad_rows(q_ref[0, :, head_cols(h)] * ATTN_SCALE, pad16).astype(BF16)
        if mode == "moba":
            ksum_scr[...] = jnp.zeros(ksum_scr.shape, F32)

    @pl.when(ph == 0)
    def _():
        for i in range(pp):
            page = g * pp + i
            for h in range(N_HEADS):
                kh = _head_slab(k_refs[i], h).astype(BF16)
                s = lax.dot_general(q16_scr[h], kh, NT_DIMS, preferred_element_type=F32)[0:t_new]
                if mode == "fox":
                    s = s - ck_ref[h:h + 1, i * PAGE_SIZE:(i + 1) * PAGE_SIZE]
                s_scr[head_rows(h), page_lanes(page)] = s
            if mode == "moba":
                for grp_i in range(HEAD_GROUPS):
                    heads = slice(grp_i * HEAD_GROUP, (grp_i + 1) * HEAD_GROUP)
                    ksum_scr[page // (MOBA_BLOCK // PAGE_SIZE), heads, :] += jnp.sum(k_refs[i][grp_i][...], axis=0)

    @pl.when((ph == 0) & (g == n_groups - 1))
    def _():
        cqs, gates = [], []
        for h in range(N_HEADS):
            kn = _pad_rows(kn_ref[0, :, head_cols(h)], LANES).astype(BF16)
            s_new = lax.dot_general(q16_scr[h], kn, NT_DIMS, preferred_element_type=F32)[0:t_new]
            if mode == "fox":
                c_new = cn_ref[h:h + 1, :]
                cqs.append(jnp.sum(jnp.where(lane == sub, jnp.broadcast_to(c_new, (t_new, LANES)), 0.0),
                                   axis=1, keepdims=True))
                s_new = s_new - c_new
            else:
                kmean = ksum_scr[:, h, :] / MOBA_BLOCK
                gates.append(lax.dot_general(q_ref[0, :, head_cols(h)], kmean, NT_DIMS, precision=HIGHEST,
                                             preferred_element_type=F32))
            s_scr[head_rows(h), past:past + LANES] = jnp.where(lane <= sub, s_new, -jnp.inf)

        if mode == "fox":
            cq = jnp.concatenate(cqs, axis=0)
        else:
            gate = jnp.concatenate(gates, axis=0)
            sel = _topk_mask(gate, gate == gate, MOBA_TOPK)

            def mask_block(b, _):
                sl = pl.ds(pl.multiple_of(b * MOBA_BLOCK, MOBA_BLOCK), MOBA_BLOCK)
                s_scr[:, sl] = jnp.where(_column(sel, b) > 0.5, s_scr[:, sl], -jnp.inf)
                return 0
            lax.fori_loop(0, past // MOBA_BLOCK, mask_block, 0)

        def run_max(c, m):
            sl = pl.ds(pl.multiple_of(c * LANES, LANES), LANES)
            s = s_scr[:, sl]
            if mode == "fox":
                s = s + cq
                s_scr[:, sl] = s
            return jnp.maximum(m, s)
        m = lax.fori_loop(0, n_chunks, run_max, jnp.full((n_rows, LANES), -jnp.inf, F32))
        m = jnp.max(m, axis=1, keepdims=True)

        def run_exp(c, l):
            sl = pl.ds(pl.multiple_of(c * LANES, LANES), LANES)
            p = jnp.exp(s_scr[:, sl] - m)
            s_scr[:, sl] = p
            return l + p
        l = lax.fori_loop(0, n_chunks, run_exp, jnp.zeros((n_rows, LANES), F32))
        l_scr[...] = jnp.sum(l, axis=1, keepdims=True)
        acc_scr[...] = jnp.zeros(acc_scr.shape, F32)

    def pv(h, lanes, v_slab):
        p = _pad_rows(s_scr[head_rows(h), lanes], pad16).astype(BF16)
        return jnp.dot(p, v_slab.astype(BF16), preferred_element_type=F32)[0:t_new]

    @pl.when(ph == 1)
    def _():
        for i in range(pp):
            page = g * pp + i
            for h in range(N_HEADS):
                acc_scr[head_rows(h), :] += pv(h, page_lanes(page), _head_slab(v_refs[i], h))

    @pl.when((ph == 1) & (g == n_groups - 1))
    def _():
        for h in range(N_HEADS):
            o = acc_scr[head_rows(h), :] + pv(h, slice(past, past + LANES),
                                              _pad_rows(vn_ref[0, :, head_cols(h)], LANES))
            o_ref[0, :, head_cols(h)] = (o / l_scr[head_rows(h), :]).astype(BF16)


def _sample_attn(q32, k32, v32, cum, cache_k, cache_v, page_table, slot, *, mode):
    bs, n_pages = page_table.shape
    t_new = q32.shape[0] // bs
    pp = PAGES_PER_STEP
    n_groups = n_pages // pp
    n_rows = N_HEADS * t_new
    past = n_pages * PAGE_SIZE

    def grp(ph, g, first):
        return jnp.where(ph == 0, g, n_groups - 1) if first else jnp.where(ph == 0, 0, g)

    def page_map(i, grp_i, first):
        return lambda b, ph, g, pt: (slot, pt[b, grp(ph, g, first) * pp + i], 0, grp_i, 0, 0)

    cache_k = cache_k.reshape(cache_k.shape[:3] + (HEAD_GROUPS, HEAD_GROUP, HEAD_DIM))
    cache_v = cache_v.reshape(cache_v.shape[:3] + (HEAD_GROUPS, HEAD_GROUP, HEAD_DIM))
    new_spec = pl.BlockSpec((1, t_new, D_MODEL), lambda b, ph, g, pt: (b, 0, 0))
    page_block = (None, None, PAGE_SIZE, None, HEAD_GROUP, HEAD_DIM)
    page_slots = [(i, grp_i) for i in range(pp) for grp_i in range(HEAD_GROUPS)]
    in_specs = [new_spec, new_spec, new_spec]
    args = [q32.reshape(bs, t_new, D_MODEL), k32.reshape(bs, t_new, D_MODEL), v32.reshape(bs, t_new, D_MODEL)]
    scratch = [pltpu.VMEM((n_rows, past + LANES), F32),
               pltpu.VMEM((n_rows, HEAD_DIM), F32),
               pltpu.VMEM((n_rows, 1), F32),
               pltpu.VMEM((N_HEADS, 2 * t_new, HEAD_DIM), BF16)]
    if mode == "fox":
        in_specs += [pl.BlockSpec((None, N_HEADS, pp * PAGE_SIZE), lambda b, ph, g, pt: (b, 0, grp(ph, g, True))),
                     pl.BlockSpec((None, N_HEADS, PAGE_SIZE), lambda b, ph, g, pt: (b, 0, 0))]
        args += list(cum)
    else:
        scratch.append(pltpu.VMEM((past // MOBA_BLOCK, N_HEADS, HEAD_DIM), F32))
    in_specs += [pl.BlockSpec(page_block, page_map(i, grp_i, True)) for i, grp_i in page_slots]
    args += [cache_k] * len(page_slots)
    in_specs += [pl.BlockSpec(page_block, page_map(i, grp_i, False)) for i, grp_i in page_slots]
    args += [cache_v] * len(page_slots)
    out = pl.pallas_call(
        functools.partial(_sample_attn_kernel, mode=mode, n_groups=n_groups, t_new=t_new),
        grid_spec=pltpu.PrefetchScalarGridSpec(
            num_scalar_prefetch=1, grid=(bs, 2, n_groups), in_specs=in_specs,
            out_specs=pl.BlockSpec((1, t_new, D_MODEL), lambda b, ph, g, pt: (b, 0, 0)),
            scratch_shapes=scratch),
        out_shape=jax.ShapeDtypeStruct((bs, t_new, D_MODEL), BF16),
        compiler_params=_params("arbitrary", "arbitrary", "arbitrary"), name=f"sample_attn_{mode}")(
            page_table, *args)
    return out.reshape(bs * t_new, D_MODEL)


def _oproj_kernel(o_ref, w_ref, x_ref, y_ref):
    y_ref[...] = x_ref[...] + jnp.dot(o_ref[...], w_ref[...], preferred_element_type=F32)


def _oproj(o16, w16, layer, x, *, tm, tn=512):
    m = x.shape[0]
    return pl.pallas_call(
        _oproj_kernel, grid=(m // tm, D_MODEL // tn),
        in_specs=[pl.BlockSpec((tm, D_MODEL), lambda i, n: (i, 0)),
                  pl.BlockSpec((None, D_MODEL, tn), lambda i, n: (layer, 0, n)),
                  pl.BlockSpec((tm, tn), lambda i, n: (i, n))],
        out_specs=pl.BlockSpec((tm, tn), lambda i, n: (i, n)),
        out_shape=jax.ShapeDtypeStruct((m, D_MODEL), F32),
        compiler_params=_params("arbitrary", "arbitrary"), name="oproj")(o16, w16, x)


def _conv_gate(gx_scr, g, cw_ref, cb_ref, g1_fix=None, g2_fix=None):
    tm = g.shape[0]
    gx_scr[HALO:HALO + tm, :] = g
    g1 = gx_scr[HALO - 1:HALO - 1 + tm, :]
    g2 = gx_scr[HALO - 2:HALO - 2 + tm, :]
    if g1_fix is not None:
        g1, g2 = g1_fix(g1), g2_fix(g2)
    acc = cb_ref[...] + g2 * cw_ref[0:1, :]
    acc = acc + g1 * cw_ref[1:2, :]
    return acc + g * cw_ref[2:3, :]


def _ffn_prompt_kernel(x_ref, xp_ref, gn_ref, wg_ref, wu_ref, cw_ref, cb_ref, wd_ref, y_ref, gt_ref,
                       h_scr, hp_scr, gx_scr, a_scr, *, tiles_per_seq):
    i, f = pl.program_id(0), pl.program_id(1)
    tm = x_ref.shape[0]
    last = pl.num_programs(1) - 1

    @pl.when(f == 0)
    def _():
        h_scr[...] = _rms_rows(x_ref[...], gn_ref[...]).astype(BF16)
        hp_scr[...] = _rms_rows(xp_ref[...], gn_ref[...]).astype(BF16)
        y_ref[...] = x_ref[...]

    def down_previous_tile():
        y_ref[...] += jnp.dot(a_scr[...], wd_ref[...], preferred_element_type=F32)

    def up_tile():
        g = jnp.dot(h_scr[...], wg_ref[...], preferred_element_type=F32)
        u = jnp.dot(h_scr[...], wu_ref[...], preferred_element_type=F32)
        g_prev = jnp.dot(hp_scr[...], wg_ref[...], preferred_element_type=F32)
        gx_scr[0:HALO, :] = jnp.where(i % tiles_per_seq == 0, 0.0, g_prev)
        a = _conv_gate(gx_scr, g, cw_ref, cb_ref)
        a_scr[...] = ((a * jax.nn.sigmoid(a)) * u).astype(BF16)
        gt_ref[0] = g[tm - 8:tm, :]

    @pl.when(f == 0)
    def _():
        up_tile()

    @pl.when((f > 0) & (f < last))
    def _():
        down_previous_tile()
        up_tile()

    @pl.when(f == last)
    def _():
        down_previous_tile()


def _ffn_prompt(x, gn, wg16, wu16, cw, cb, wd16, layer, *, seq, tm=1024, tf=256):
    m = x.shape[0]
    nf = D_FF // tf
    hb = tm // HALO

    def up(f):
        return jnp.minimum(f, nf - 1)

    y, gt = pl.pallas_call(
        functools.partial(_ffn_prompt_kernel, tiles_per_seq=seq // tm),
        grid=(m // tm, nf + 1),
        in_specs=[pl.BlockSpec((tm, D_MODEL), lambda i, f: (i, 0)),
                  pl.BlockSpec((HALO, D_MODEL), lambda i, f: (jnp.maximum(i * hb - 1, 0), 0)),
                  pl.BlockSpec((1, D_MODEL), lambda i, f: (0, 0)),
                  pl.BlockSpec((None, D_MODEL, tf), lambda i, f: (layer, 0, up(f))),
                  pl.BlockSpec((None, D_MODEL, tf), lambda i, f: (layer, 0, up(f))),
                  pl.BlockSpec((None, CONV_W, tf), lambda i, f: (layer, 0, up(f))),
                  pl.BlockSpec((None, 1, tf), lambda i, f: (layer, 0, up(f))),
                  pl.BlockSpec((None, tf, D_MODEL), lambda i, f: (layer, jnp.maximum(f - 1, 0), 0))],
        out_specs=[pl.BlockSpec((tm, D_MODEL), lambda i, f: (i, 0)),
                   pl.BlockSpec((1, 8, tf), lambda i, f: (i, 0, up(f)))],
        out_shape=[jax.ShapeDtypeStruct((m, D_MODEL), F32),
                   jax.ShapeDtypeStruct((m // tm, 8, D_FF), F32)],
        scratch_shapes=[pltpu.VMEM((tm, D_MODEL), BF16), pltpu.VMEM((HALO, D_MODEL), BF16),
                        pltpu.VMEM((tm + HALO, tf), F32), pltpu.VMEM((tm, tf), BF16)],
        compiler_params=_params("arbitrary", "arbitrary"), name="ffn_prompt")(
            x, x, gn.reshape(1, D_MODEL), wg16, wu16, cw, cb.reshape(cb.shape[0], 1, D_FF), wd16)
    tiles = seq // tm
    tail = gt.reshape(m // seq, tiles, 8, D_FF)[:, -1, 8 - (CONV_W - 1):, :]
    return y, tail


def _ffn_sample_kernel(x_ref, gn_ref, wg_ref, wu_ref, cw_ref, cb_ref, wd_ref, s1_ref, s2_ref, y_ref, g_ref,
                       h_scr, acc_scr, gx_scr, *, t_new):
    f = pl.program_id(0)

    @pl.when(f == 0)
    def _():
        h_scr[...] = _rms_rows(x_ref[...], gn_ref[...]).astype(BF16)
        acc_scr[...] = jnp.zeros(acc_scr.shape, F32)
        gx_scr[...] = jnp.zeros(gx_scr.shape, F32)

    g = jnp.dot(h_scr[...], wg_ref[...], preferred_element_type=F32)
    u = jnp.dot(h_scr[...], wu_ref[...], preferred_element_type=F32)
    tok = lax.broadcasted_iota(jnp.int32, g.shape, 0) % t_new
    a = _conv_gate(gx_scr, g, cw_ref, cb_ref,
                   g1_fix=lambda g1: jnp.where(tok >= 1, g1, s1_ref[...]),
                   g2_fix=lambda g2: jnp.where(tok >= 2, g2, s2_ref[...]))
    a = (a * jax.nn.sigmoid(a)) * u
    acc_scr[...] += jnp.dot(a.astype(BF16), wd_ref[...], preferred_element_type=F32)
    g_ref[...] = g

    @pl.when(f == pl.num_programs(0) - 1)
    def _():
        y_ref[...] = x_ref[...] + acc_scr[...]


def _ffn_sample(x, gn, wg16, wu16, cw, cb, wd16, layer, state, *, tf=512):
    m = x.shape[0]
    bs = state.shape[0]
    t_new = m // bs
    s1 = jnp.broadcast_to(state[:, -1:, :], (bs, t_new, D_FF)).reshape(m, D_FF)
    s2 = jnp.tile(state, (1, t_new // (CONV_W - 1), 1)).reshape(m, D_FF)
    y, g = pl.pallas_call(
        functools.partial(_ffn_sample_kernel, t_new=t_new),
        grid=(D_FF // tf,),
        in_specs=[pl.BlockSpec((m, D_MODEL), lambda f: (0, 0)),
                  pl.BlockSpec((1, D_MODEL), lambda f: (0, 0)),
                  pl.BlockSpec((None, D_MODEL, tf), lambda f: (layer, 0, f)),
                  pl.BlockSpec((None, D_MODEL, tf), lambda f: (layer, 0, f)),
                  pl.BlockSpec((None, CONV_W, tf), lambda f: (layer, 0, f)),
                  pl.BlockSpec((None, 1, tf), lambda f: (layer, 0, f)),
                  pl.BlockSpec((None, tf, D_MODEL), lambda f: (layer, f, 0)),
                  pl.BlockSpec((m, tf), lambda f: (0, f)),
                  pl.BlockSpec((m, tf), lambda f: (0, f))],
        out_specs=[pl.BlockSpec((m, D_MODEL), lambda f: (0, 0)),
                   pl.BlockSpec((m, tf), lambda f: (0, f))],
        out_shape=[jax.ShapeDtypeStruct((m, D_MODEL), F32), jax.ShapeDtypeStruct((m, D_FF), F32)],
        scratch_shapes=[pltpu.VMEM((m, D_MODEL), BF16), pltpu.VMEM((m, D_MODEL), F32),
                        pltpu.VMEM((m + HALO, tf), F32)],
        compiler_params=_params("arbitrary"), name="ffn_sample")(
            x, gn.reshape(1, D_MODEL), wg16, wu16, cw, cb.reshape(cb.shape[0], 1, D_FF), wd16, s1, s2)
    return y, g.reshape(bs, t_new, D_FF)[:, t_new - (CONV_W - 1):, :]


def _rope_tables(pos):
    half = HEAD_DIM // 2
    inv_freq = ROPE_THETA ** (-jnp.arange(half, dtype=F32) / half)
    ang = pos.astype(F32)[:, None] * inv_freq[None, :]
    cos, sin = jnp.cos(ang), jnp.sin(ang)
    return jnp.concatenate([cos, cos], axis=-1), jnp.concatenate([-sin, sin], axis=-1)


def kernel(x_prompt, x_sample, cache_fox_k, cache_fox_v, cache_fox_logf, cache_moba_k, cache_moba_v,
           state_ffn_conv, page_table, attn_norm, ffn_norm, fox_w_in, fox_b_f, fox_q_norm, fox_k_norm, fox_w_o,
           moba_w_in, moba_q_norm, moba_k_norm, moba_w_o, ffn_w_gate, ffn_w_up, ffn_conv_w, ffn_conv_b,
           ffn_w_down):
    B, T, _ = x_prompt.shape
    Bs, Ts, _ = x_sample.shape
    depth = attn_norm.shape[0]
    past_len = page_table.shape[1] * PAGE_SIZE
    mp, ms = B * T, Bs * Ts
    tm_p = 1024

    fox_w16 = fox_w_in.astype(BF16)
    fox_wf16 = fox_w16[:, :, 3 * D_MODEL:]
    moba_w16 = moba_w_in.astype(BF16)
    fox_wo16, moba_wo16 = fox_w_o.astype(BF16), moba_w_o.astype(BF16)
    wg16, wu16, wd16 = ffn_w_gate.astype(BF16), ffn_w_up.astype(BF16), ffn_w_down.astype(BF16)

    cos_p, sin_p = _rope_tables(jnp.arange(T, dtype=jnp.int32))
    cos_s, sin_s = _rope_tables(past_len + jnp.arange(Ts, dtype=jnp.int32))
    cos_s, sin_s = jnp.tile(cos_s, (Bs, 1)), jnp.tile(sin_s, (Bs, 1))

    xp, xs = x_prompt.reshape(mp, D_MODEL), x_sample.reshape(ms, D_MODEL)
    outs = {k: [] for k in ("fkp", "fvp", "flp", "fks", "fvs", "fls", "mkp", "mvp", "mks", "mvs", "cvp", "cvs")}
    heads_p = (B, T, N_HEADS, HEAD_DIM)
    heads_s = (Bs, Ts, N_HEADS, HEAD_DIM)
    for i in range(depth):
        j = i // 2
        if i % 2 == 0:
            w16, gq, gk = fox_w16, fox_q_norm[j], fox_k_norm[j]
            qp = _proj(xp, attn_norm[i], w16, j, 0, gain=gq, tm=tm_p)
            kp, kp16 = _proj(xp, attn_norm[i], w16, j, D_MODEL, gain=gk, emit_bf16=True, tm=tm_p)
            vp, vp16 = _proj(xp, attn_norm[i], w16, j, 2 * D_MODEL, emit_bf16=True, tm=tm_p)
            lp = _logf_proj(xp, attn_norm[i], fox_wf16[j], fox_b_f[j], tm=tm_p)
            qs = _proj(xs, attn_norm[i], w16, j, 0, gain=gq, tm=ms)
            ks = _proj(xs, attn_norm[i], w16, j, D_MODEL, gain=gk, tm=ms)
            vs = _proj(xs, attn_norm[i], w16, j, 2 * D_MODEL, tm=ms)
            ls = _logf_proj(xs, attn_norm[i], fox_wf16[j], fox_b_f[j], tm=ms)
            lp_rows = lp.reshape(B, T, N_HEADS).transpose(0, 2, 1).reshape(B * N_HEADS, T)
            c_rows = _cumsum_rows(lp_rows).reshape(B * N_HEADS, 1, T)
            op = _prompt_attn(qp, kp16, vp16, c_rows, mode="fox", batch=B, seq=T)
            cum_s = _past_cum(ls, cache_fox_logf, page_table, j)
            os_ = _sample_attn(qs, ks, vs, cum_s, cache_fox_k, cache_fox_v, page_table, j, mode="fox")
            wo16 = fox_wo16
            outs["fkp"].append(kp.reshape(heads_p)); outs["fvp"].append(vp.reshape(heads_p))
            outs["flp"].append(lp.reshape(B, T, N_HEADS))
            outs["fks"].append(ks.reshape(heads_s)); outs["fvs"].append(vs.reshape(heads_s))
            outs["fls"].append(ls.reshape(Bs, Ts, N_HEADS))
        else:
            w16, gq, gk = moba_w16, moba_q_norm[j], moba_k_norm[j]
            qp = _proj(xp, attn_norm[i], w16, j, 0, gain=gq, cos=cos_p, sin=sin_p, tm=tm_p)
            kp, kp16 = _proj(xp, attn_norm[i], w16, j, D_MODEL, gain=gk, cos=cos_p, sin=sin_p,
                             emit_bf16=True, tm=tm_p)
            vp, vp16 = _proj(xp, attn_norm[i], w16, j, 2 * D_MODEL, emit_bf16=True, tm=tm_p)
            qs = _proj(xs, attn_norm[i], w16, j, 0, gain=gq, cos=cos_s, sin=sin_s, tm=ms)
            ks = _proj(xs, attn_norm[i], w16, j, D_MODEL, gain=gk, cos=cos_s, sin=sin_s, tm=ms)
            vs = _proj(xs, attn_norm[i], w16, j, 2 * D_MODEL, tm=ms)
            kmean = _block_mean(kp).reshape(B, T // MOBA_BLOCK, D_MODEL)
            op = _prompt_attn(qp, kp16, vp16, kmean, mode="moba", batch=B, seq=T)
            os_ = _sample_attn(qs, ks, vs, None, cache_moba_k, cache_moba_v, page_table, j, mode="moba")
            wo16 = moba_wo16
            outs["mkp"].append(kp.reshape(heads_p)); outs["mvp"].append(vp.reshape(heads_p))
            outs["mks"].append(ks.reshape(heads_s)); outs["mvs"].append(vs.reshape(heads_s))
        xp = _oproj(op, wo16, j, xp, tm=tm_p)
        xs = _oproj(os_, wo16, j, xs, tm=ms)
        xp, sp = _ffn_prompt(xp, ffn_norm[i], wg16, wu16, ffn_conv_w, ffn_conv_b, wd16, i, seq=T)
        xs, ss = _ffn_sample(xs, ffn_norm[i], wg16, wu16, ffn_conv_w, ffn_conv_b, wd16, i, state_ffn_conv[i])
        outs["cvp"].append(sp); outs["cvs"].append(ss)
    st = lambda k: jnp.stack(outs[k])
    return (xp.reshape(B, T, D_MODEL), xs.reshape(Bs, Ts, D_MODEL),
            st("fkp"), st("fvp"), st("flp"), st("mkp"), st("mvp"), st("cvp"),
            st("fks"), st("fvs"), st("fls"), st("mks"), st("mvs"), st("cvs"))
```

```python
import functools

import jax
import jax.numpy as jnp
from jax import lax
from jax.experimental import pallas as pl
from jax.experimental.pallas import tpu as pltpu

D_MODEL = 2048
N_HEADS = 16
HEAD_DIM = D_MODEL // N_HEADS
D_FF = 5632
CONV_W = 3
ROPE_THETA = 10000.0
MOBA_BLOCK = 256
MOBA_TOPK = 3
PAGE_SIZE = 128
EPS = 1e-6
ATTN_SCALE = HEAD_DIM ** -0.5

LANES = 128
HALO = 16
PAGES_PER_STEP = 4
HEAD_GROUP = 8
HEAD_GROUPS = N_HEADS // HEAD_GROUP
VMEM_LIMIT = 52 * 1024 * 1024

F32 = jnp.float32
BF16 = jnp.bfloat16
HIGHEST = lax.Precision.HIGHEST
NT_DIMS = (((1,), (1,)), ((), ()))


def _params(*sem):
    return pltpu.CompilerParams(dimension_semantics=sem, vmem_limit_bytes=VMEM_LIMIT)


def _rms_rows(x, g):
    return x * lax.rsqrt(jnp.mean(x * x, axis=-1, keepdims=True) + EPS) * g


def _proj_kernel(*refs, rope, nn):
    it = iter(refs)
    x_ref, gn_ref, w_ref, gq_ref, gk_ref = (next(it) for _ in range(5))
    cos_ref, sin_ref = (next(it), next(it)) if rope else (None, None)
    q32_ref, k32_ref, k16_ref, v32_ref, v16_ref, h_scr, raw_scr = (next(it) for _ in range(7))
    n = pl.program_id(1)

    def matmul_tile():
        raw_scr[...] = jnp.dot(h_scr[...], w_ref[...], preferred_element_type=F32)

    def finish_previous_tile(gain_ref, y32_ref, y16_ref):
        for c in range(raw_scr.shape[1] // HEAD_DIM):
            sl = slice(c * HEAD_DIM, (c + 1) * HEAD_DIM)
            y = raw_scr[:, sl]
            if gain_ref is not None:
                y = _rms_rows(y, gain_ref[...])
                if rope:
                    y = y * cos_ref[...] + pltpu.roll(y, HEAD_DIM // 2, 1) * sin_ref[...]
            y32_ref[:, sl] = y
            if y16_ref is not None:
                y16_ref[:, sl] = y.astype(BF16)

    @pl.when(n == 0)
    def _():
        h_scr[...] = _rms_rows(x_ref[...], gn_ref[...]).astype(BF16)
        matmul_tile()

    @pl.when((n >= 1) & (n <= nn))
    def _():
        finish_previous_tile(gq_ref, q32_ref, None)
        matmul_tile()

    @pl.when((n > nn) & (n <= 2 * nn))
    def _():
        finish_previous_tile(gk_ref, k32_ref, k16_ref)
        matmul_tile()

    @pl.when((n > 2 * nn) & (n < 3 * nn))
    def _():
        finish_previous_tile(None, v32_ref, v16_ref)
        matmul_tile()

    @pl.when(n == 3 * nn)
    def _():
        finish_previous_tile(None, v32_ref, v16_ref)


def _proj(x, gn, w16, layer, gq, gk, *, cos=None, sin=None, tm, tn=512):
    m = x.shape[0]
    rope = cos is not None
    nn = D_MODEL // tn
    grid = (m // tm, 3 * nn + 1)

    def out_spec(first):
        return pl.BlockSpec((tm, tn), lambda i, n: (i, jnp.clip(n - 1 - first, 0, nn - 1)))

    in_specs = [pl.BlockSpec((tm, D_MODEL), lambda i, n: (i, 0)),
                pl.BlockSpec((1, D_MODEL), lambda i, n: (0, 0)),
                pl.BlockSpec((None, D_MODEL, tn), lambda i, n: (layer, 0, jnp.minimum(n, 3 * nn - 1))),
                pl.BlockSpec((1, HEAD_DIM), lambda i, n: (0, 0)),
                pl.BlockSpec((1, HEAD_DIM), lambda i, n: (0, 0))]
    args = [x, gn.reshape(1, D_MODEL), w16, gq.reshape(1, HEAD_DIM), gk.reshape(1, HEAD_DIM)]
    if rope:
        nblk = cos.shape[0] // tm
        in_specs += [pl.BlockSpec((tm, HEAD_DIM), lambda i, n: (i % nblk, 0))] * 2
        args += [cos, sin]
    f32, b16 = jax.ShapeDtypeStruct((m, D_MODEL), F32), jax.ShapeDtypeStruct((m, D_MODEL), BF16)
    return pl.pallas_call(
        functools.partial(_proj_kernel, rope=rope, nn=nn),
        grid=grid, in_specs=in_specs,
        out_specs=[out_spec(0), out_spec(nn), out_spec(nn), out_spec(2 * nn), out_spec(2 * nn)],
        out_shape=[f32, f32, b16, f32, b16],
        scratch_shapes=[pltpu.VMEM((tm, D_MODEL), BF16), pltpu.VMEM((tm, tn), F32)],
        compiler_params=_params("arbitrary", "arbitrary"), name="proj")(*args)


def _logf_kernel(x_ref, gn_ref, w_ref, b_ref, o_ref):
    h = _rms_rows(x_ref[...], gn_ref[...]).astype(BF16)
    z = jnp.dot(h, w_ref[...], preferred_element_type=F32) + b_ref[...]
    o_ref[...] = jnp.minimum(z, 0.0) - jnp.log1p(jnp.exp(-jnp.abs(z)))


def _logf_proj(x, gn, wf16, bf, *, tm):
    m = x.shape[0]
    return pl.pallas_call(
        _logf_kernel, grid=(m // tm,),
        in_specs=[pl.BlockSpec((tm, D_MODEL), lambda i: (i, 0)),
                  pl.BlockSpec((1, D_MODEL), lambda i: (0, 0)),
                  pl.BlockSpec((D_MODEL, N_HEADS), lambda i: (0, 0)),
                  pl.BlockSpec((1, N_HEADS), lambda i: (0, 0))],
        out_specs=pl.BlockSpec((tm, N_HEADS), lambda i: (i, 0)),
        out_shape=jax.ShapeDtypeStruct((m, N_HEADS), F32),
        compiler_params=_params("arbitrary"), name="logf_proj")(
            x, gn.reshape(1, D_MODEL), wf16, bf.reshape(1, N_HEADS))


def _upper_ones(n):
    r = lax.broadcasted_iota(jnp.int32, (n, n), 0)
    c = lax.broadcasted_iota(jnp.int32, (n, n), 1)
    return (r <= c).astype(F32)


def _cumsum_rows_kernel(x_ref, o_ref, *, tk):
    u = _upper_ones(tk)
    carry = jnp.zeros((x_ref.shape[0], 1), F32)
    for s in range(x_ref.shape[1] // tk):
        c = jnp.dot(x_ref[:, s * tk:(s + 1) * tk], u, precision=HIGHEST, preferred_element_type=F32) + carry
        o_ref[:, s * tk:(s + 1) * tk] = c
        carry = c[:, tk - 1:tk]


def _cumsum_rows(x):
    return pl.pallas_call(
        functools.partial(_cumsum_rows_kernel, tk=512),
        out_shape=jax.ShapeDtypeStruct(x.shape, F32), name="cumsum_rows")(x)


def _block_mean_kernel(k_ref, o_ref):
    o_ref[0] = jnp.mean(k_ref[...], axis=0, keepdims=True)


def _block_mean(k32):
    nb = k32.shape[0] // MOBA_BLOCK
    return pl.pallas_call(
        _block_mean_kernel, grid=(nb,),
        in_specs=[pl.BlockSpec((MOBA_BLOCK, D_MODEL), lambda i: (i, 0))],
        out_specs=pl.BlockSpec((1, 1, D_MODEL), lambda i: (i, 0, 0)),
        out_shape=jax.ShapeDtypeStruct((nb, 1, D_MODEL), F32),
        compiler_params=_params("arbitrary"), name="block_mean")(k32)


def _topk_mask(gate, valid, k, axis=1):
    idx = lax.broadcasted_iota(jnp.int32, gate.shape, axis)
    n = gate.shape[axis]
    g = jnp.where(valid, gate, -jnp.inf)
    sel = jnp.zeros(gate.shape, F32)
    for _ in range(k):
        mx = jnp.max(g, axis=axis, keepdims=True)
        is_max = (g == mx) & (g > -jnp.inf)
        first = jnp.min(jnp.where(is_max, idx, n), axis=axis, keepdims=True)
        pick = idx == first
        sel = jnp.where(pick, 1.0, sel)
        g = jnp.where(pick, -jnp.inf, g)
    return sel


def _column(mat, j):
    idx = lax.broadcasted_iota(jnp.int32, mat.shape, 1)
    return jnp.sum(jnp.where(idx == j, mat, 0.0), axis=1, keepdims=True)


def _prompt_attn_kernel(*refs, mode, tq):
    it = iter(refs)
    q_ref, k_ref, v_ref, aux_ref = next(it), next(it), next(it), next(it)
    cq_ref = next(it) if mode == "fox" else None
    o_ref, s_scr, mx_scr, l_scr, acc_scr = (next(it) for _ in range(5))
    qi = pl.program_id(2)
    tk = tq
    nbt = tq // MOBA_BLOCK
    q = q_ref[...]
    qs = (q * ATTN_SCALE).astype(BF16)
    if mode == "fox":
        row = lax.broadcasted_iota(jnp.int32, (tq, tk), 0)
        col = lax.broadcasted_iota(jnp.int32, (tq, tk), 1)
        cq = _column(cq_ref[...], pl.program_id(1))
    else:
        kmean = aux_ref[0]
        gate = lax.dot_general(kmean, q, NT_DIMS, precision=HIGHEST, preferred_element_type=F32)
        blk = lax.broadcasted_iota(jnp.int32, gate.shape, 0)
        own_g = qi * nbt + lax.broadcasted_iota(jnp.int32, gate.shape, 1) // MOBA_BLOCK
        sel = jnp.transpose(_pad_rows(_topk_mask(gate, blk < own_g, MOBA_TOPK, axis=0), LANES))
        r_p = lax.broadcasted_iota(jnp.int32, (tq, MOBA_BLOCK), 0)
        c_p = lax.broadcasted_iota(jnp.int32, (tq, MOBA_BLOCK), 1)
    mx_scr[...] = jnp.full(mx_scr.shape, -jnp.inf, F32)

    def logits(j):
        k0 = pl.multiple_of(j * tk, tk)
        s = lax.dot_general(qs, k_ref[pl.ds(k0, tk), :], NT_DIMS, preferred_element_type=F32)
        if mode == "fox":
            s = s + (cq - aux_ref[0, :, pl.ds(k0, tk)])
        return s

    def put(js, ss):
        m = mx_scr[...]
        for j, s in zip(js, ss):
            s_scr[:, pl.ds(pl.multiple_of(j * tk, tk), tk)] = s
            for c in range(tk // LANES):
                m = jnp.maximum(m, s[:, c * LANES:(c + 1) * LANES])
        mx_scr[...] = m

    def past_tiles(js):
        ss = []
        for j in js:
            s = logits(j)
            if mode == "moba":
                s = jnp.concatenate(
                    [jnp.where(_column(sel, j * nbt + c) > 0.5, s[:, c * MOBA_BLOCK:(c + 1) * MOBA_BLOCK],
                               -jnp.inf) for c in range(nbt)], axis=1)
            ss.append(s)
        put(js, ss)

    def past_pair(jj, _):
        past_tiles([2 * jj, 2 * jj + 1])
        return 0

    lax.fori_loop(0, qi // 2, past_pair, 0)

    @pl.when(qi % 2 == 1)
    def _():
        past_tiles([qi - 1])

    s = logits(qi)
    if mode == "fox":
        s = jnp.where(col <= row, s, -jnp.inf)
    else:
        pieces = []
        for c in range(nbt):
            rb = r_p // MOBA_BLOCK
            own = (rb == c) & (c_p <= r_p - c * MOBA_BLOCK)
            picked = (rb > c) & (_column(sel, qi * nbt + c) > 0.5)
            pieces.append(jnp.where(own | picked, s[:, c * MOBA_BLOCK:(c + 1) * MOBA_BLOCK], -jnp.inf))
        s = jnp.concatenate(pieces, axis=1)
    put([qi], [s])

    m = jnp.max(mx_scr[...], axis=1, keepdims=True)
    l_scr[...] = jnp.zeros(l_scr.shape, F32)
    acc_scr[...] = jnp.zeros(acc_scr.shape, F32)

    def pv_tiles(js):
        l = l_scr[...]
        o = None
        for j in js:
            k0 = pl.multiple_of(j * tk, tk)
            p = jnp.exp(s_scr[:, pl.ds(k0, tk)] - m)
            for c in range(tk // LANES):
                l = l + p[:, c * LANES:(c + 1) * LANES]
            pv = jnp.dot(p.astype(BF16), v_ref[pl.ds(k0, tk), :], preferred_element_type=F32)
            o = pv if o is None else o + pv
        l_scr[...] = l
        acc_scr[...] += o

    def pv_pair(jj, _):
        pv_tiles([2 * jj, 2 * jj + 1])
        return 0

    lax.fori_loop(0, (qi + 1) // 2, pv_pair, 0)

    @pl.when(qi % 2 == 0)
    def _():
        pv_tiles([qi])

    o_ref[...] = (acc_scr[...] / jnp.sum(l_scr[...], axis=1, keepdims=True)).astype(BF16)


def _prompt_attn(q32, k16, v16, aux, *, mode, batch, seq, tq=512):
    nq = seq // tq
    if mode == "fox":
        c_rows = aux.reshape(batch * N_HEADS, 1, seq)
        c_cols = aux.reshape(batch, N_HEADS, seq).transpose(0, 2, 1).reshape(batch * seq, N_HEADS)
        aux_specs = [pl.BlockSpec((1, 1, seq), lambda b, h, i: (b * N_HEADS + h, 0, 0)),
                     pl.BlockSpec((tq, N_HEADS), lambda b, h, i: (b * nq + i, 0))]
        aux_args = [c_rows, c_cols]
    else:
        aux_specs = [pl.BlockSpec((1, seq // MOBA_BLOCK, HEAD_DIM), lambda b, h, i: (b, 0, h))]
        aux_args = [aux]
    return pl.pallas_call(
        functools.partial(_prompt_attn_kernel, mode=mode, tq=tq),
        grid=(batch, N_HEADS, nq),
        in_specs=[pl.BlockSpec((tq, HEAD_DIM), lambda b, h, i: (b * nq + i, h)),
                  pl.BlockSpec((seq, HEAD_DIM), lambda b, h, i: (b, h)),
                  pl.BlockSpec((seq, HEAD_DIM), lambda b, h, i: (b, h))] + aux_specs,
        out_specs=pl.BlockSpec((tq, HEAD_DIM), lambda b, h, i: (b * nq + i, h)),
        out_shape=jax.ShapeDtypeStruct((batch * seq, D_MODEL), BF16),
        scratch_shapes=[pltpu.VMEM((tq, seq), F32),
                        pltpu.VMEM((tq, LANES), F32),
                        pltpu.VMEM((tq, LANES), F32),
                        pltpu.VMEM((tq, HEAD_DIM), F32)],
        compiler_params=_params("arbitrary", "arbitrary", "arbitrary"), name=f"prompt_attn_{mode}")(
            q32, k16, v16, *aux_args)


def _pad_rows(x, n):
    return jnp.concatenate([x, jnp.zeros((n - x.shape[0], x.shape[1]), x.dtype)], axis=0)


def _head_slab(page_refs, h):
    ref = page_refs[h // HEAD_GROUP].reshape(PAGE_SIZE * HEAD_GROUP, HEAD_DIM)
    return ref[pl.ds(h % HEAD_GROUP, PAGE_SIZE, stride=HEAD_GROUP), :]


def _past_cum_kernel(*refs, n_groups, ppc):
    it = iter(refs)
    next(it)
    lfn_ref = next(it)
    lf_refs = [next(it) for _ in range(ppc)]
    ck_ref, cn_ref, carry_scr = next(it), next(it), next(it)
    g = pl.program_id(1)
    upper = _upper_ones(PAGE_SIZE)
    eye = (lax.broadcasted_iota(jnp.int32, (N_HEADS, N_HEADS), 0)
           == lax.broadcasted_iota(jnp.int32, (N_HEADS, N_HEADS), 1)).astype(F32)

    def page_cums(lf):
        lf_t = lax.dot_general(eye, lf, NT_DIMS, precision=HIGHEST, preferred_element_type=F32)
        return [jnp.dot(lf_t[:, i * PAGE_SIZE:(i + 1) * PAGE_SIZE], upper, precision=HIGHEST,
                        preferred_element_type=F32) for i in range(lf.shape[0] // PAGE_SIZE)]

    @pl.when(g == 0)
    def _():
        carry_scr[...] = jnp.zeros(carry_scr.shape, F32)

    carry = carry_scr[...]
    for i, c in enumerate(page_cums(jnp.concatenate([r[...] for r in lf_refs], axis=0))):
        c = c + carry
        ck_ref[:, i * PAGE_SIZE:(i + 1) * PAGE_SIZE] = c
        carry = c[:, PAGE_SIZE - 1:PAGE_SIZE]
    carry_scr[...] = carry

    @pl.when(g == n_groups - 1)
    def _():
        cn_ref[...] = page_cums(_pad_rows(lfn_ref[0], PAGE_SIZE))[0] + carry


def _past_cum(lf_new, cache_lf, page_table, slot, *, ppc=8):
    bs, n_pages = page_table.shape
    t_new = lf_new.shape[0] // bs
    n_groups = n_pages // ppc

    def lf_map(i):
        return lambda b, g, pt: (slot, pt[b, g * ppc + i], 0, 0)

    return pl.pallas_call(
        functools.partial(_past_cum_kernel, n_groups=n_groups, ppc=ppc),
        grid_spec=pltpu.PrefetchScalarGridSpec(
            num_scalar_prefetch=1, grid=(bs, n_groups),
            in_specs=[pl.BlockSpec((1, t_new, N_HEADS), lambda b, g, pt: (b, 0, 0))]
            + [pl.BlockSpec((None, None, PAGE_SIZE, N_HEADS), lf_map(i)) for i in range(ppc)],
            out_specs=[pl.BlockSpec((None, N_HEADS, ppc * PAGE_SIZE), lambda b, g, pt: (b, 0, g)),
                       pl.BlockSpec((None, N_HEADS, PAGE_SIZE), lambda b, g, pt: (b, 0, 0))],
            scratch_shapes=[pltpu.VMEM((N_HEADS, 1), F32)]),
        out_shape=[jax.ShapeDtypeStruct((bs, N_HEADS, n_pages * PAGE_SIZE), F32),
                   jax.ShapeDtypeStruct((bs, N_HEADS, PAGE_SIZE), F32)],
        compiler_params=_params("arbitrary", "arbitrary"), name="past_cum")(
            page_table, lf_new.reshape(bs, t_new, N_HEADS), *([cache_lf] * ppc))


def _sample_attn_kernel(*refs, mode, n_groups, t_new):
    pp = PAGES_PER_STEP
    it = iter(refs)
    next(it)
    q_ref, kn_ref, vn_ref = next(it), next(it), next(it)
    ck_ref, cn_ref = (next(it), next(it)) if mode == "fox" else (None, None)
    k_refs = [[next(it) for _ in range(HEAD_GROUPS)] for _ in range(pp)]
    v_refs = [[next(it) for _ in range(HEAD_GROUPS)] for _ in range(pp)]
    o_ref = next(it)
    s_scr, acc_scr, l_scr, q16_scr = next(it), next(it), next(it), next(it)
    ksum_scr = next(it) if mode == "moba" else None

    ph, g = pl.program_id(1), pl.program_id(2)
    n_rows = N_HEADS * t_new
    past = n_groups * pp * PAGE_SIZE
    n_chunks = past // LANES + 1
    sub = lax.broadcasted_iota(jnp.int32, (t_new, LANES), 0)
    lane = lax.broadcasted_iota(jnp.int32, (t_new, LANES), 1)
    pad16 = 2 * t_new

    def head_cols(h):
        return slice(h * HEAD_DIM, (h + 1) * HEAD_DIM)

    def head_rows(h):
        return slice(h * t_new, (h + 1) * t_new)

    def page_lanes(page):
        return pl.ds(pl.multiple_of(page * PAGE_SIZE, PAGE_SIZE), PAGE_SIZE)

    @pl.when((ph == 0) & (g == 0))
    def _():
        for h in range(N_HEADS):
            q16_scr[h] = _pad_rows(q_ref[0, :, head_cols(h)] * ATTN_SCALE, pad16).astype(BF16)
        if mode == "moba":
            ksum_scr[...] = jnp.zeros(ksum_scr.shape, F32)

    @pl.when(ph == 0)
    def _():
        for i in range(pp):
            page = g * pp + i
            for h in range(N_HEADS):
                kh = _head_slab(k_refs[i], h).astype(BF16)
                s = lax.dot_general(q16_scr[h], kh, NT_DIMS, preferred_element_type=F32)[0:t_new]
                if mode == "fox":
                    s = s - ck_ref[h:h + 1, i * PAGE_SIZE:(i + 1) * PAGE_SIZE]
                s_scr[head_rows(h), page_lanes(page)] = s
            if mode == "moba":
                for grp_i in range(HEAD_GROUPS):
                    heads = slice(grp_i * HEAD_GROUP, (grp_i + 1) * HEAD_GROUP)
                    ksum_scr[page // (MOBA_BLOCK // PAGE_SIZE), heads, :] += jnp.sum(k_refs[i][grp_i][...], axis=0)

    @pl.when((ph == 0) & (g == n_groups - 1))
    def _():
        cqs, gates = [], []
        for h in range(N_HEADS):
            kn = _pad_rows(kn_ref[0, :, head_cols(h)], LANES).astype(BF16)
            s_new = lax.dot_general(q16_scr[h], kn, NT_DIMS, preferred_element_type=F32)[0:t_new]
            if mode == "fox":
                c_new = cn_ref[h:h + 1, :]
                cqs.append(jnp.sum(jnp.where(lane == sub, jnp.broadcast_to(c_new, (t_new, LANES)), 0.0),
                                   axis=1, keepdims=True))
                s_new = s_new - c_new
            else:
                kmean = ksum_scr[:, h, :] / MOBA_BLOCK
                gates.append(lax.dot_general(q_ref[0, :, head_cols(h)], kmean, NT_DIMS, precision=HIGHEST,
                                             preferred_element_type=F32))
            s_scr[head_rows(h), past:past + LANES] = jnp.where(lane <= sub, s_new, -jnp.inf)

        if mode == "fox":
            cq = jnp.concatenate(cqs, axis=0)
        else:
            gate = jnp.concatenate(gates, axis=0)
            sel = _topk_mask(gate, gate == gate, MOBA_TOPK)

            def mask_block(b, _):
                sl = pl.ds(pl.multiple_of(b * MOBA_BLOCK, MOBA_BLOCK), MOBA_BLOCK)
                s_scr[:, sl] = jnp.where(_column(sel, b) > 0.5, s_scr[:, sl], -jnp.inf)
                return 0
            lax.fori_loop(0, past // MOBA_BLOCK, mask_block, 0)

        def run_max(c, m):
            sl = pl.ds(pl.multiple_of(c * LANES, LANES), LANES)
            s = s_scr[:, sl]
            if mode == "fox":
                s = s + cq
                s_scr[:, sl] = s
            return jnp.maximum(m, s)
        m = lax.fori_loop(0, n_chunks, run_max, jnp.full((n_rows, LANES), -jnp.inf, F32))
        m = jnp.max(m, axis=1, keepdims=True)

        def run_exp(c, l):
            sl = pl.ds(pl.multiple_of(c * LANES, LANES), LANES)
            p = jnp.exp(s_scr[:, sl] - m)
            s_scr[:, sl] = p
            return l + p
        l = lax.fori_loop(0, n_chunks, run_exp, jnp.zeros((n_rows, LANES), F32))
        l_scr[...] = jnp.sum(l, axis=1, keepdims=True)
        acc_scr[...] = jnp.zeros(acc_scr.shape, F32)

    def pv(h, lanes, v_slab):
        p = _pad_rows(s_scr[head_rows(h), lanes], pad16).astype(BF16)
        return jnp.dot(p, v_slab.astype(BF16), preferred_element_type=F32)[0:t_new]

    @pl.when(ph == 1)
    def _():
        for i in range(pp):
            page = g * pp + i
            for h in range(N_HEADS):
                acc_scr[head_rows(h), :] += pv(h, page_lanes(page), _head_slab(v_refs[i], h))

    @pl.when((ph == 1) & (g == n_groups - 1))
    def _():
        for h in range(N_HEADS):
            o = acc_scr[head_rows(h), :] + pv(h, slice(past, past + LANES),
                                              _pad_rows(vn_ref[0, :, head_cols(h)], LANES))
            o_ref[0, :, head_cols(h)] = (o / l_scr[head_rows(h), :]).astype(BF16)


def _sample_attn(q32, k32, v32, cum, cache_k, cache_v, page_table, slot, *, mode):
    bs, n_pages = page_table.shape
    t_new = q32.shape[0] // bs
    pp = PAGES_PER_STEP
    n_groups = n_pages // pp
    n_rows = N_HEADS * t_new
    past = n_pages * PAGE_SIZE

    def grp(ph, g, first):
        return jnp.where(ph == 0, g, n_groups - 1) if first else jnp.where(ph == 0, 0, g)

    def page_map(i, grp_i, first):
        return lambda b, ph, g, pt: (slot, pt[b, grp(ph, g, first) * pp + i], 0, grp_i, 0, 0)

    cache_k = cache_k.reshape(cache_k.shape[:3] + (HEAD_GROUPS, HEAD_GROUP, HEAD_DIM))
    cache_v = cache_v.reshape(cache_v.shape[:3] + (HEAD_GROUPS, HEAD_GROUP, HEAD_DIM))
    new_spec = pl.BlockSpec((1, t_new, D_MODEL), lambda b, ph, g, pt: (b, 0, 0))
    page_block = (None, None, PAGE_SIZE, None, HEAD_GROUP, HEAD_DIM)
    page_slots = [(i, grp_i) for i in range(pp) for grp_i in range(HEAD_GROUPS)]
    in_specs = [new_spec, new_spec, new_spec]
    args = [q32.reshape(bs, t_new, D_MODEL), k32.reshape(bs, t_new, D_MODEL), v32.reshape(bs, t_new, D_MODEL)]
    scratch = [pltpu.VMEM((n_rows, past + LANES), F32),
               pltpu.VMEM((n_rows, HEAD_DIM), F32),
               pltpu.VMEM((n_rows, 1), F32),
               pltpu.VMEM((N_HEADS, 2 * t_new, HEAD_DIM), BF16)]
    if mode == "fox":
        in_specs += [pl.BlockSpec((None, N_HEADS, pp * PAGE_SIZE), lambda b, ph, g, pt: (b, 0, grp(ph, g, True))),
                     pl.BlockSpec((None, N_HEADS, PAGE_SIZE), lambda b, ph, g, pt: (b, 0, 0))]
        args += list(cum)
    else:
        scratch.append(pltpu.VMEM((past // MOBA_BLOCK, N_HEADS, HEAD_DIM), F32))
    in_specs += [pl.BlockSpec(page_block, page_map(i, grp_i, True)) for i, grp_i in page_slots]
    args += [cache_k] * len(page_slots)
    in_specs += [pl.BlockSpec(page_block, page_map(i, grp_i, False)) for i, grp_i in page_slots]
    args += [cache_v] * len(page_slots)
    out = pl.pallas_call(
        functools.partial(_sample_attn_kernel, mode=mode, n_groups=n_groups, t_new=t_new),
        grid_spec=pltpu.PrefetchScalarGridSpec(
            num_scalar_prefetch=1, grid=(bs, 2, n_groups), in_specs=in_specs,
            out_specs=pl.BlockSpec((1, t_new, D_MODEL), lambda b, ph, g, pt: (b, 0, 0)),
            scratch_shapes=scratch),
        out_shape=jax.ShapeDtypeStruct((bs, t_new, D_MODEL), BF16),
        compiler_params=_params("arbitrary", "arbitrary", "arbitrary"), name=f"sample_attn_{mode}")(
            page_table, *args)
    return out.reshape(bs * t_new, D_MODEL)


def _oproj_kernel(o_ref, w_ref, x_ref, y_ref):
    y_ref[...] = x_ref[...] + jnp.dot(o_ref[...], w_ref[...], preferred_element_type=F32)


def _oproj(o16, w16, layer, x, *, tm, tn=512):
    m = x.shape[0]
    return pl.pallas_call(
        _oproj_kernel, grid=(m // tm, D_MODEL // tn),
        in_specs=[pl.BlockSpec((tm, D_MODEL), lambda i, n: (i, 0)),
                  pl.BlockSpec((None, D_MODEL, tn), lambda i, n: (layer, 0, n)),
                  pl.BlockSpec((tm, tn), lambda i, n: (i, n))],
        out_specs=pl.BlockSpec((tm, tn), lambda i, n: (i, n)),
        out_shape=jax.ShapeDtypeStruct((m, D_MODEL), F32),
        compiler_params=_params("arbitrary", "arbitrary"), name="oproj")(o16, w16, x)


def _conv_gate(gx_scr, g, cw_ref, cb_ref, g1_fix=None, g2_fix=None):
    tm = g.shape[0]
    gx_scr[HALO:HALO + tm, :] = g
    g1 = gx_scr[HALO - 1:HALO - 1 + tm, :]
    g2 = gx_scr[HALO - 2:HALO - 2 + tm, :]
    if g1_fix is not None:
        g1, g2 = g1_fix(g1), g2_fix(g2)
    acc = cb_ref[...] + g2 * cw_ref[0:1, :]
    acc = acc + g1 * cw_ref[1:2, :]
    return acc + g * cw_ref[2:3, :]


def _ffn_prompt_kernel(x_ref, xp_ref, gn_ref, wg_ref, wu_ref, cw_ref, cb_ref, wd_ref, y_ref, gt_ref,
                       h_scr, hp_scr, gx_scr, a_scr, *, tiles_per_seq):
    i, f = pl.program_id(0), pl.program_id(1)
    tm = x_ref.shape[0]
    last = pl.num_programs(1) - 1

    @pl.when(f == 0)
    def _():
        h_scr[...] = _rms_rows(x_ref[...], gn_ref[...]).astype(BF16)
        hp_scr[...] = _rms_rows(xp_ref[...], gn_ref[...]).astype(BF16)
        y_ref[...] = x_ref[...]

    def down_previous_tile():
        y_ref[...] += jnp.dot(a_scr[...], wd_ref[...], preferred_element_type=F32)

    def up_tile():
        g = jnp.dot(h_scr[...], wg_ref[...], preferred_element_type=F32)
        u = jnp.dot(h_scr[...], wu_ref[...], preferred_element_type=F32)
        g_prev = jnp.dot(hp_scr[...], wg_ref[...], preferred_element_type=F32)
        gx_scr[0:HALO, :] = jnp.where(i % tiles_per_seq == 0, 0.0, g_prev)
        a = _conv_gate(gx_scr, g, cw_ref, cb_ref)
        a_scr[...] = ((a * jax.nn.sigmoid(a)) * u).astype(BF16)
        gt_ref[0] = g[tm - 8:tm, :]

    @pl.when(f == 0)
    def _():
        up_tile()

    @pl.when((f > 0) & (f < last))
    def _():
        down_previous_tile()
        up_tile()

    @pl.when(f == last)
    def _():
        down_previous_tile()


def _ffn_prompt(x, gn, wg16, wu16, cw, cb, wd16, layer, *, seq, tm=1024, tf=256):
    m = x.shape[0]
    nf = D_FF // tf
    hb = tm // HALO

    def up(f):
        return jnp.minimum(f, nf - 1)

    y, gt = pl.pallas_call(
        functools.partial(_ffn_prompt_kernel, tiles_per_seq=seq // tm),
        grid=(m // tm, nf + 1),
        in_specs=[pl.BlockSpec((tm, D_MODEL), lambda i, f: (i, 0)),
                  pl.BlockSpec((HALO, D_MODEL), lambda i, f: (jnp.maximum(i * hb - 1, 0), 0)),
                  pl.BlockSpec((1, D_MODEL), lambda i, f: (0, 0)),
                  pl.BlockSpec((None, D_MODEL, tf), lambda i, f: (layer, 0, up(f))),
                  pl.BlockSpec((None, D_MODEL, tf), lambda i, f: (layer, 0, up(f))),
                  pl.BlockSpec((None, CONV_W, tf), lambda i, f: (layer, 0, up(f))),
                  pl.BlockSpec((None, 1, tf), lambda i, f: (layer, 0, up(f))),
                  pl.BlockSpec((None, tf, D_MODEL), lambda i, f: (layer, jnp.maximum(f - 1, 0), 0))],
        out_specs=[pl.BlockSpec((tm, D_MODEL), lambda i, f: (i, 0)),
                   pl.BlockSpec((1, 8, tf), lambda i, f: (i, 0, up(f)))],
        out_shape=[jax.ShapeDtypeStruct((m, D_MODEL), F32),
                   jax.ShapeDtypeStruct((m // tm, 8, D_FF), F32)],
        scratch_shapes=[pltpu.VMEM((tm, D_MODEL), BF16), pltpu.VMEM((HALO, D_MODEL), BF16),
                        pltpu.VMEM((tm + HALO, tf), F32), pltpu.VMEM((tm, tf), BF16)],
        compiler_params=_params("arbitrary", "arbitrary"), name="ffn_prompt")(
            x, x, gn.reshape(1, D_MODEL), wg16, wu16, cw, cb.reshape(cb.shape[0], 1, D_FF), wd16)
    tiles = seq // tm
    tail = gt.reshape(m // seq, tiles, 8, D_FF)[:, -1, 8 - (CONV_W - 1):, :]
    return y, tail


def _ffn_sample_kernel(x_ref, gn_ref, wg_ref, wu_ref, cw_ref, cb_ref, wd_ref, s1_ref, s2_ref, y_ref, g_ref,
                       h_scr, acc_scr, gx_scr, *, t_new):
    f = pl.program_id(0)

    @pl.when(f == 0)
    def _():
        h_scr[...] = _rms_rows(x_ref[...], gn_ref[...]).astype(BF16)
        acc_scr[...] = jnp.zeros(acc_scr.shape, F32)
        gx_scr[...] = jnp.zeros(gx_scr.shape, F32)

    g = jnp.dot(h_scr[...], wg_ref[...], preferred_element_type=F32)
    u = jnp.dot(h_scr[...], wu_ref[...], preferred_element_type=F32)
    tok = lax.broadcasted_iota(jnp.int32, g.shape, 0) % t_new
    a = _conv_gate(gx_scr, g, cw_ref, cb_ref,
                   g1_fix=lambda g1: jnp.where(tok >= 1, g1, s1_ref[...]),
                   g2_fix=lambda g2: jnp.where(tok >= 2, g2, s2_ref[...]))
    a = (a * jax.nn.sigmoid(a)) * u
    acc_scr[...] += jnp.dot(a.astype(BF16), wd_ref[...], preferred_element_type=F32)
    g_ref[...] = g

    @pl.when(f == pl.num_programs(0) - 1)
    def _():
        y_ref[...] = x_ref[...] + acc_scr[...]


def _ffn_sample(x, gn, wg16, wu16, cw, cb, wd16, layer, state, *, tf=512):
    m = x.shape[0]
    bs = state.shape[0]
    t_new = m // bs
    s1 = jnp.broadcast_to(state[:, -1:, :], (bs, t_new, D_FF)).reshape(m, D_FF)
    s2 = jnp.tile(state, (1, t_new // (CONV_W - 1), 1)).reshape(m, D_FF)
    y, g = pl.pallas_call(
        functools.partial(_ffn_sample_kernel, t_new=t_new),
        grid=(D_FF // tf,),
        in_specs=[pl.BlockSpec((m, D_MODEL), lambda f: (0, 0)),
                  pl.BlockSpec((1, D_MODEL), lambda f: (0, 0)),
                  pl.BlockSpec((None, D_MODEL, tf), lambda f: (layer, 0, f)),
                  pl.BlockSpec((None, D_MODEL, tf), lambda f: (layer, 0, f)),
                  pl.BlockSpec((None, CONV_W, tf), lambda f: (layer, 0, f)),
                  pl.BlockSpec((None, 1, tf), lambda f: (layer, 0, f)),
                  pl.BlockSpec((None, tf, D_MODEL), lambda f: (layer, f, 0)),
                  pl.BlockSpec((m, tf), lambda f: (0, f)),
                  pl.BlockSpec((m, tf), lambda f: (0, f))],
        out_specs=[pl.BlockSpec((m, D_MODEL), lambda f: (0, 0)),
                   pl.BlockSpec((m, tf), lambda f: (0, f))],
        out_shape=[jax.ShapeDtypeStruct((m, D_MODEL), F32), jax.ShapeDtypeStruct((m, D_FF), F32)],
        scratch_shapes=[pltpu.VMEM((m, D_MODEL), BF16), pltpu.VMEM((m, D_MODEL), F32),
                        pltpu.VMEM((m + HALO, tf), F32)],
        compiler_params=_params("arbitrary"), name="ffn_sample")(
            x, gn.reshape(1, D_MODEL), wg16, wu16, cw, cb.reshape(cb.shape[0], 1, D_FF), wd16, s1, s2)
    return y, g.reshape(bs, t_new, D_FF)[:, t_new - (CONV_W - 1):, :]


def _rope_tables(pos):
    half = HEAD_DIM // 2
    inv_freq = ROPE_THETA ** (-jnp.arange(half, dtype=F32) / half)
    ang = pos.astype(F32)[:, None] * inv_freq[None, :]
    cos, sin = jnp.cos(ang), jnp.sin(ang)
    return jnp.concatenate([cos, cos], axis=-1), jnp.concatenate([-sin, sin], axis=-1)


def kernel(x_prompt, x_sample, cache_fox_k, cache_fox_v, cache_fox_logf, cache_moba_k, cache_moba_v,
           state_ffn_conv, page_table, attn_norm, ffn_norm, fox_w_in, fox_b_f, fox_q_norm, fox_k_norm, fox_w_o,
           moba_w_in, moba_q_norm, moba_k_norm, moba_w_o, ffn_w_gate, ffn_w_up, ffn_conv_w, ffn_conv_b,
           ffn_w_down):
    B, T, _ = x_prompt.shape
    Bs, Ts, _ = x_sample.shape
    depth = attn_norm.shape[0]
    past_len = page_table.shape[1] * PAGE_SIZE
    mp, ms = B * T, Bs * Ts
    tm_p = 1024

    fox_w16 = fox_w_in.astype(BF16)
    fox_wf16 = fox_w16[:, :, 3 * D_MODEL:]
    moba_w16 = moba_w_in.astype(BF16)
    fox_wo16, moba_wo16 = fox_w_o.astype(BF16), moba_w_o.astype(BF16)
    wg16, wu16, wd16 = ffn_w_gate.astype(BF16), ffn_w_up.astype(BF16), ffn_w_down.astype(BF16)

    cos_p, sin_p = _rope_tables(jnp.arange(T, dtype=jnp.int32))
    cos_s, sin_s = _rope_tables(past_len + jnp.arange(Ts, dtype=jnp.int32))
    cos_s, sin_s = jnp.tile(cos_s, (Bs, 1)), jnp.tile(sin_s, (Bs, 1))

    xp, xs = x_prompt.reshape(mp, D_MODEL), x_sample.reshape(ms, D_MODEL)
    outs = {k: [] for k in ("fkp", "fvp", "flp", "fks", "fvs", "fls", "mkp", "mvp", "mks", "mvs", "cvp", "cvs")}
    heads_p = (B, T, N_HEADS, HEAD_DIM)
    heads_s = (Bs, Ts, N_HEADS, HEAD_DIM)
    for i in range(depth):
        j = i // 2
        if i % 2 == 0:
            w16, gq, gk = fox_w16, fox_q_norm[j], fox_k_norm[j]
            qp, kp, kp16, vp, vp16 = _proj(xp, attn_norm[i], w16, j, gq, gk, tm=tm_p)
            lp = _logf_proj(xp, attn_norm[i], fox_wf16[j], fox_b_f[j], tm=tm_p)
            qs, ks, _, vs, _ = _proj(xs, attn_norm[i], w16, j, gq, gk, tm=ms)
            ls = _logf_proj(xs, attn_norm[i], fox_wf16[j], fox_b_f[j], tm=ms)
            lp_rows = lp.reshape(B, T, N_HEADS).transpose(0, 2, 1).reshape(B * N_HEADS, T)
            c_rows = _cumsum_rows(lp_rows)
            op = _prompt_attn(qp, kp16, vp16, c_rows, mode="fox", batch=B, seq=T)
            cum_s = _past_cum(ls, cache_fox_logf, page_table, j)
            os_ = _sample_attn(qs, ks, vs, cum_s, cache_fox_k, cache_fox_v, page_table, j, mode="fox")
            wo16 = fox_wo16
            outs["fkp"].append(kp.reshape(heads_p)); outs["fvp"].append(vp.reshape(heads_p))
            outs["flp"].append(lp.reshape(B, T, N_HEADS))
            outs["fks"].append(ks.reshape(heads_s)); outs["fvs"].append(vs.reshape(heads_s))
            outs["fls"].append(ls.reshape(Bs, Ts, N_HEADS))
        else:
            w16, gq, gk = moba_w16, moba_q_norm[j], moba_k_norm[j]
            qp, kp, kp16, vp, vp16 = _proj(xp, attn_norm[i], w16, j, gq, gk, cos=cos_p, sin=sin_p, tm=tm_p)
            qs, ks, _, vs, _ = _proj(xs, attn_norm[i], w16, j, gq, gk, cos=cos_s, sin=sin_s, tm=ms)
            kmean = _block_mean(kp).reshape(B, T // MOBA_BLOCK, D_MODEL)
            op = _prompt_attn(qp, kp16, vp16, kmean, mode="moba", batch=B, seq=T)
            os_ = _sample_attn(qs, ks, vs, None, cache_moba_k, cache_moba_v, page_table, j, mode="moba")
            wo16 = moba_wo16
            outs["mkp"].append(kp.reshape(heads_p)); outs["mvp"].append(vp.reshape(heads_p))
            outs["mks"].append(ks.reshape(heads_s)); outs["mvs"].append(vs.reshape(heads_s))
        xp = _oproj(op, wo16, j, xp, tm=tm_p)
        xs = _oproj(os_, wo16, j, xs, tm=ms)
        xp, sp = _ffn_prompt(xp, ffn_norm[i], wg16, wu16, ffn_conv_w, ffn_conv_b, wd16, i, seq=T)
        xs, ss = _ffn_sample(xs, ffn_norm[i], wg16, wu16, ffn_conv_w, ffn_conv_b, wd16, i, state_ffn_conv[i])
        outs["cvp"].append(sp); outs["cvs"].append(ss)
    st = lambda k: jnp.stack(outs[k])
    return (xp.reshape(B, T, D_MODEL), xs.reshape(Bs, Ts, D_MODEL),
            st("fkp"), st("fvp"), st("flp"), st("mkp"), st("mvp"), st("cvp"),
            st("fks"), st("fvs"), st("fls"), st("mks"), st("mvs"), st("cvs"))
```

```python
import functools

import jax
import jax.numpy as jnp
from jax import lax
from jax.experimental import pallas as pl
from jax.experimental.pallas import tpu as pltpu

D_MODEL = 2048
N_HEADS = 16
HEAD_DIM = D_MODEL // N_HEADS
D_FF = 5632
CONV_W = 3
ROPE_THETA = 10000.0
MOBA_BLOCK = 256
MOBA_TOPK = 3
PAGE_SIZE = 128
EPS = 1e-6
ATTN_SCALE = HEAD_DIM ** -0.5
LOG2E = 1.4426950408889634

LANES = 128
HALO = 16
PAGES_PER_STEP = 8
HEAD_GROUP = 8
HEAD_GROUPS = N_HEADS // HEAD_GROUP
VMEM_LIMIT = 52 * 1024 * 1024

F32 = jnp.float32
BF16 = jnp.bfloat16
HIGHEST = lax.Precision.HIGHEST
NT_DIMS = (((1,), (1,)), ((), ()))


def _params(*sem):
    return pltpu.CompilerParams(dimension_semantics=sem, vmem_limit_bytes=VMEM_LIMIT)


def _rms_rows(x, g):
    return x * lax.rsqrt(jnp.mean(x * x, axis=-1, keepdims=True) + EPS) * g


def _proj_kernel(*refs, rope, nn):
    it = iter(refs)
    x_ref, gn_ref, w_ref, gq_ref, gk_ref = (next(it) for _ in range(5))
    cos_ref, sin_ref = (next(it), next(it)) if rope else (None, None)
    q32_ref, k32_ref, k16_ref, v32_ref, v16_ref, h_scr, raw_scr = (next(it) for _ in range(7))
    n = pl.program_id(1)

    def matmul_tile():
        raw_scr[...] = jnp.dot(h_scr[...], w_ref[...], preferred_element_type=F32)

    def finish_previous_tile(gain_ref, y32_ref, y16_ref):
        for c in range(raw_scr.shape[1] // HEAD_DIM):
            sl = slice(c * HEAD_DIM, (c + 1) * HEAD_DIM)
            y = raw_scr[:, sl]
            if gain_ref is not None:
                y = _rms_rows(y, gain_ref[...])
                if rope:
                    y = y * cos_ref[...] + pltpu.roll(y, HEAD_DIM // 2, 1) * sin_ref[...]
            y32_ref[:, sl] = y
            if y16_ref is not None:
                y16_ref[:, sl] = y.astype(BF16)

    @pl.when(n == 0)
    def _():
        h_scr[...] = _rms_rows(x_ref[...], gn_ref[...]).astype(BF16)
        matmul_tile()

    @pl.when((n >= 1) & (n <= nn))
    def _():
        finish_previous_tile(gq_ref, q32_ref, None)
        matmul_tile()

    @pl.when((n > nn) & (n <= 2 * nn))
    def _():
        finish_previous_tile(gk_ref, k32_ref, k16_ref)
        matmul_tile()

    @pl.when((n > 2 * nn) & (n < 3 * nn))
    def _():
        finish_previous_tile(None, v32_ref, v16_ref)
        matmul_tile()

    @pl.when(n == 3 * nn)
    def _():
        finish_previous_tile(None, v32_ref, v16_ref)


def _proj(x, gn, w16, layer, gq, gk, *, cos=None, sin=None, tm, tn=512):
    m = x.shape[0]
    rope = cos is not None
    nn = D_MODEL // tn
    grid = (m // tm, 3 * nn + 1)

    def out_spec(first):
        return pl.BlockSpec((tm, tn), lambda i, n: (i, jnp.clip(n - 1 - first, 0, nn - 1)))

    in_specs = [pl.BlockSpec((tm, D_MODEL), lambda i, n: (i, 0)),
                pl.BlockSpec((1, D_MODEL), lambda i, n: (0, 0)),
                pl.BlockSpec((None, D_MODEL, tn), lambda i, n: (layer, 0, jnp.minimum(n, 3 * nn - 1))),
                pl.BlockSpec((1, HEAD_DIM), lambda i, n: (0, 0)),
                pl.BlockSpec((1, HEAD_DIM), lambda i, n: (0, 0))]
    args = [x, gn.reshape(1, D_MODEL), w16, gq.reshape(1, HEAD_DIM), gk.reshape(1, HEAD_DIM)]
    if rope:
        nblk = cos.shape[0] // tm
        in_specs += [pl.BlockSpec((tm, HEAD_DIM), lambda i, n: (i % nblk, 0))] * 2
        args += [cos, sin]
    f32, b16 = jax.ShapeDtypeStruct((m, D_MODEL), F32), jax.ShapeDtypeStruct((m, D_MODEL), BF16)
    return pl.pallas_call(
        functools.partial(_proj_kernel, rope=rope, nn=nn),
        grid=grid, in_specs=in_specs,
        out_specs=[out_spec(0), out_spec(nn), out_spec(nn), out_spec(2 * nn), out_spec(2 * nn)],
        out_shape=[f32, f32, b16, f32, b16],
        scratch_shapes=[pltpu.VMEM((tm, D_MODEL), BF16), pltpu.VMEM((tm, tn), F32)],
        compiler_params=_params("arbitrary", "arbitrary"), name="proj")(*args)


def _logf_kernel(x_ref, gn_ref, w_ref, b_ref, o_ref):
    h = _rms_rows(x_ref[...], gn_ref[...]).astype(BF16)
    z = jnp.dot(h, w_ref[...], preferred_element_type=F32) + b_ref[...]
    o_ref[...] = jnp.minimum(z, 0.0) - jnp.log1p(jnp.exp(-jnp.abs(z)))


def _logf_proj(x, gn, wf16, bf, *, tm):
    m = x.shape[0]
    return pl.pallas_call(
        _logf_kernel, grid=(m // tm,),
        in_specs=[pl.BlockSpec((tm, D_MODEL), lambda i: (i, 0)),
                  pl.BlockSpec((1, D_MODEL), lambda i: (0, 0)),
                  pl.BlockSpec((D_MODEL, N_HEADS), lambda i: (0, 0)),
                  pl.BlockSpec((1, N_HEADS), lambda i: (0, 0))],
        out_specs=pl.BlockSpec((tm, N_HEADS), lambda i: (i, 0)),
        out_shape=jax.ShapeDtypeStruct((m, N_HEADS), F32),
        compiler_params=_params("arbitrary"), name="logf_proj")(
            x, gn.reshape(1, D_MODEL), wf16, bf.reshape(1, N_HEADS))


def _upper_ones(n):
    r = lax.broadcasted_iota(jnp.int32, (n, n), 0)
    c = lax.broadcasted_iota(jnp.int32, (n, n), 1)
    return (r <= c).astype(F32)


def _cumsum_rows_kernel(x_ref, o_ref, *, tk):
    u = _upper_ones(tk)
    carry = jnp.zeros((x_ref.shape[0], 1), F32)
    for s in range(x_ref.shape[1] // tk):
        c = jnp.dot(x_ref[:, s * tk:(s + 1) * tk], u, precision=HIGHEST, preferred_element_type=F32) + carry
        o_ref[:, s * tk:(s + 1) * tk] = c
        carry = c[:, tk - 1:tk]


def _cumsum_rows(x):
    return pl.pallas_call(
        functools.partial(_cumsum_rows_kernel, tk=512),
        out_shape=jax.ShapeDtypeStruct(x.shape, F32), name="cumsum_rows")(x)


def _block_mean_kernel(k_ref, o_ref):
    o_ref[0] = jnp.mean(k_ref[...], axis=0, keepdims=True)


def _block_mean(k32):
    nb = k32.shape[0] // MOBA_BLOCK
    return pl.pallas_call(
        _block_mean_kernel, grid=(nb,),
        in_specs=[pl.BlockSpec((MOBA_BLOCK, D_MODEL), lambda i: (i, 0))],
        out_specs=pl.BlockSpec((1, 1, D_MODEL), lambda i: (i, 0, 0)),
        out_shape=jax.ShapeDtypeStruct((nb, 1, D_MODEL), F32),
        compiler_params=_params("arbitrary"), name="block_mean")(k32)


def _topk_mask(gate, valid, k, axis=1):
    idx = lax.broadcasted_iota(jnp.int32, gate.shape, axis)
    n = gate.shape[axis]
    g = jnp.where(valid, gate, -jnp.inf)
    sel = jnp.zeros(gate.shape, F32)
    for _ in range(k):
        mx = jnp.max(g, axis=axis, keepdims=True)
        is_max = (g == mx) & (g > -jnp.inf)
        first = jnp.min(jnp.where(is_max, idx, n), axis=axis, keepdims=True)
        pick = idx == first
        sel = jnp.where(pick, 1.0, sel)
        g = jnp.where(pick, -jnp.inf, g)
    return sel


def _column(mat, j):
    idx = lax.broadcasted_iota(jnp.int32, mat.shape, 1)
    return jnp.sum(jnp.where(idx == j, mat, 0.0), axis=1, keepdims=True)


def _prompt_attn_kernel(*refs, mode, tq):
    it = iter(refs)
    q_ref, k_ref, v_ref, aux_ref = next(it), next(it), next(it), next(it)
    cq_ref = next(it) if mode == "fox" else None
    o_ref, s_scr, mx_scr, l_scr, acc_scr = (next(it) for _ in range(5))
    qi = pl.program_id(2)
    tk = tq
    nbt = tq // MOBA_BLOCK
    q = q_ref[...]
    qs = (q * (ATTN_SCALE * LOG2E)).astype(BF16)
    if mode == "fox":
        row = lax.broadcasted_iota(jnp.int32, (tq, tk), 0)
        col = lax.broadcasted_iota(jnp.int32, (tq, tk), 1)
        cq = _column(cq_ref[...], pl.program_id(1)) * LOG2E
    else:
        kmean = aux_ref[0]
        gate = lax.dot_general(kmean, q, NT_DIMS, precision=HIGHEST, preferred_element_type=F32)
        blk = lax.broadcasted_iota(jnp.int32, gate.shape, 0)
        own_g = qi * nbt + lax.broadcasted_iota(jnp.int32, gate.shape, 1) // MOBA_BLOCK
        sel = jnp.transpose(_pad_rows(_topk_mask(gate, blk < own_g, MOBA_TOPK, axis=0), LANES))
        r_p = lax.broadcasted_iota(jnp.int32, (tq, MOBA_BLOCK), 0)
        c_p = lax.broadcasted_iota(jnp.int32, (tq, MOBA_BLOCK), 1)
    mx_scr[...] = jnp.full(mx_scr.shape, -jnp.inf, F32)

    def logits(j):
        k0 = pl.multiple_of(j * tk, tk)
        s = lax.dot_general(qs, k_ref[pl.ds(k0, tk), :], NT_DIMS, preferred_element_type=F32)
        if mode == "fox":
            s = s + (cq - aux_ref[0, :, pl.ds(k0, tk)] * LOG2E)
        return s

    def put(js, ss):
        m = mx_scr[...]
        for j, s in zip(js, ss):
            s_scr[:, pl.ds(pl.multiple_of(j * tk, tk), tk)] = s
            for c in range(tk // LANES):
                m = jnp.maximum(m, s[:, c * LANES:(c + 1) * LANES])
        mx_scr[...] = m

    def past_tiles(js):
        ss = []
        for j in js:
            s = logits(j)
            if mode == "moba":
                s = jnp.concatenate(
                    [jnp.where(_column(sel, j * nbt + c) > 0.5, s[:, c * MOBA_BLOCK:(c + 1) * MOBA_BLOCK],
                               -jnp.inf) for c in range(nbt)], axis=1)
            ss.append(s)
        put(js, ss)

    def past_pair(jj, _):
        past_tiles([2 * jj, 2 * jj + 1])
        return 0

    lax.fori_loop(0, qi // 2, past_pair, 0)

    @pl.when(qi % 2 == 1)
    def _():
        past_tiles([qi - 1])

    s = logits(qi)
    if mode == "fox":
        s = jnp.where(col <= row, s, -jnp.inf)
    else:
        pieces = []
        for c in range(nbt):
            rb = r_p // MOBA_BLOCK
            own = (rb == c) & (c_p <= r_p - c * MOBA_BLOCK)
            picked = (rb > c) & (_column(sel, qi * nbt + c) > 0.5)
            pieces.append(jnp.where(own | picked, s[:, c * MOBA_BLOCK:(c + 1) * MOBA_BLOCK], -jnp.inf))
        s = jnp.concatenate(pieces, axis=1)
    put([qi], [s])

    m = jnp.max(mx_scr[...], axis=1, keepdims=True)
    l_scr[...] = jnp.zeros(l_scr.shape, F32)
    acc_scr[...] = jnp.zeros(acc_scr.shape, F32)

    def pv_tiles(js):
        l = l_scr[...]
        o = None
        for j in js:
            k0 = pl.multiple_of(j * tk, tk)
            p = jnp.exp2(s_scr[:, pl.ds(k0, tk)] - m)
            for c in range(tk // LANES):
                l = l + p[:, c * LANES:(c + 1) * LANES]
            pv = jnp.dot(p.astype(BF16), v_ref[pl.ds(k0, tk), :], preferred_element_type=F32)
            o = pv if o is None else o + pv
        l_scr[...] = l
        acc_scr[...] += o

    def pv_pair(jj, _):
        pv_tiles([2 * jj, 2 * jj + 1])
        return 0

    lax.fori_loop(0, (qi + 1) // 2, pv_pair, 0)

    @pl.when(qi % 2 == 0)
    def _():
        pv_tiles([qi])

    o_ref[...] = (acc_scr[...] / jnp.sum(l_scr[...], axis=1, keepdims=True)).astype(BF16)


def _prompt_attn(q32, k16, v16, aux, *, mode, batch, seq, tq=512):
    nq = seq // tq
    if mode == "fox":
        c_rows = aux.reshape(batch * N_HEADS, 1, seq)
        c_cols = aux.reshape(batch, N_HEADS, seq).transpose(0, 2, 1).reshape(batch * seq, N_HEADS)
        aux_specs = [pl.BlockSpec((1, 1, seq), lambda b, h, i: (b * N_HEADS + h, 0, 0)),
                     pl.BlockSpec((tq, N_HEADS), lambda b, h, i: (b * nq + i, 0))]
        aux_args = [c_rows, c_cols]
    else:
        aux_specs = [pl.BlockSpec((1, seq // MOBA_BLOCK, HEAD_DIM), lambda b, h, i: (b, 0, h))]
        aux_args = [aux]
    return pl.pallas_call(
        functools.partial(_prompt_attn_kernel, mode=mode, tq=tq),
        grid=(batch, N_HEADS, nq),
        in_specs=[pl.BlockSpec((tq, HEAD_DIM), lambda b, h, i: (b * nq + i, h)),
                  pl.BlockSpec((seq, HEAD_DIM), lambda b, h, i: (b, h)),
                  pl.BlockSpec((seq, HEAD_DIM), lambda b, h, i: (b, h))] + aux_specs,
        out_specs=pl.BlockSpec((tq, HEAD_DIM), lambda b, h, i: (b * nq + i, h)),
        out_shape=jax.ShapeDtypeStruct((batch * seq, D_MODEL), BF16),
        scratch_shapes=[pltpu.VMEM((tq, seq), F32),
                        pltpu.VMEM((tq, LANES), F32),
                        pltpu.VMEM((tq, LANES), F32),
                        pltpu.VMEM((tq, HEAD_DIM), F32)],
        compiler_params=_params("arbitrary", "arbitrary", "arbitrary"), name=f"prompt_attn_{mode}")(
            q32, k16, v16, *aux_args)


def _pad_rows(x, n):
    return jnp.concatenate([x, jnp.zeros((n - x.shape[0], x.shape[1]), x.dtype)], axis=0)


def _head_slab(page_refs, h):
    ref = page_refs[h // HEAD_GROUP].reshape(PAGE_SIZE * HEAD_GROUP, HEAD_DIM)
    return ref[pl.ds(h % HEAD_GROUP, PAGE_SIZE, stride=HEAD_GROUP), :]


def _past_cum_kernel(*refs, n_groups, ppc):
    it = iter(refs)
    next(it)
    lfn_ref = next(it)
    lf_refs = [next(it) for _ in range(ppc)]
    ck_ref, cn_ref, carry_scr = next(it), next(it), next(it)
    g = pl.program_id(1)
    upper = _upper_ones(PAGE_SIZE)
    eye = (lax.broadcasted_iota(jnp.int32, (N_HEADS, N_HEADS), 0)
           == lax.broadcasted_iota(jnp.int32, (N_HEADS, N_HEADS), 1)).astype(F32)

    def page_cums(lf):
        lf_t = lax.dot_general(eye, lf, NT_DIMS, precision=HIGHEST, preferred_element_type=F32)
        return [jnp.dot(lf_t[:, i * PAGE_SIZE:(i + 1) * PAGE_SIZE], upper, precision=HIGHEST,
                        preferred_element_type=F32) for i in range(lf.shape[0] // PAGE_SIZE)]

    @pl.when(g == 0)
    def _():
        carry_scr[...] = jnp.zeros(carry_scr.shape, F32)

    carry = carry_scr[...]
    for i, c in enumerate(page_cums(jnp.concatenate([r[...] for r in lf_refs], axis=0))):
        c = c + carry
        ck_ref[:, i * PAGE_SIZE:(i + 1) * PAGE_SIZE] = c
        carry = c[:, PAGE_SIZE - 1:PAGE_SIZE]
    carry_scr[...] = carry

    @pl.when(g == n_groups - 1)
    def _():
        cn_ref[...] = page_cums(_pad_rows(lfn_ref[0], PAGE_SIZE))[0] + carry


def _past_cum(lf_new, cache_lf, page_table, slot, *, ppc=16):
    bs, n_pages = page_table.shape
    t_new = lf_new.shape[0] // bs
    ppc = min(ppc, n_pages)
    n_groups = n_pages // ppc

    def lf_map(i):
        return lambda b, g, pt: (slot, pt[b, g * ppc + i], 0, 0)

    return pl.pallas_call(
        functools.partial(_past_cum_kernel, n_groups=n_groups, ppc=ppc),
        grid_spec=pltpu.PrefetchScalarGridSpec(
            num_scalar_prefetch=1, grid=(bs, n_groups),
            in_specs=[pl.BlockSpec((1, t_new, N_HEADS), lambda b, g, pt: (b, 0, 0))]
            + [pl.BlockSpec((None, None, PAGE_SIZE, N_HEADS), lf_map(i)) for i in range(ppc)],
            out_specs=[pl.BlockSpec((None, N_HEADS, ppc * PAGE_SIZE), lambda b, g, pt: (b, 0, g)),
                       pl.BlockSpec((None, N_HEADS, PAGE_SIZE), lambda b, g, pt: (b, 0, 0))],
            scratch_shapes=[pltpu.VMEM((N_HEADS, 1), F32)]),
        out_shape=[jax.ShapeDtypeStruct((bs, N_HEADS, n_pages * PAGE_SIZE), F32),
                   jax.ShapeDtypeStruct((bs, N_HEADS, PAGE_SIZE), F32)],
        compiler_params=_params("arbitrary", "arbitrary"), name="past_cum")(
            page_table, lf_new.reshape(bs, t_new, N_HEADS), *([cache_lf] * ppc))


def _sample_attn_kernel(*refs, mode, n_groups, t_new):
    pp = PAGES_PER_STEP
    it = iter(refs)
    next(it)
    q_ref, kn_ref, vn_ref = next(it), next(it), next(it)
    ck_ref, cn_ref = (next(it), next(it)) if mode == "fox" else (None, None)
    k_refs = [[next(it) for _ in range(HEAD_GROUPS)] for _ in range(pp)]
    v_refs = [[next(it) for _ in range(HEAD_GROUPS)] for _ in range(pp)]
    o_ref = next(it)
    s_scr, acc_scr, l_scr, q16_scr = next(it), next(it), next(it), next(it)
    ksum_scr = next(it) if mode == "moba" else None

    ph, g = pl.program_id(1), pl.program_id(2)
    n_rows = N_HEADS * t_new
    past = n_groups * pp * PAGE_SIZE
    n_chunks = past // LANES + 1
    sub = lax.broadcasted_iota(jnp.int32, (t_new, LANES), 0)
    lane = lax.broadcasted_iota(jnp.int32, (t_new, LANES), 1)
    pad16 = 2 * t_new

    def head_cols(h):
        return slice(h * HEAD_DIM, (h + 1) * HEAD_DIM)

    def head_rows(h):
        return slice(h * t_new, (h + 1) * t_new)

    def page_lanes(page):
        return pl.ds(pl.multiple_of(page * PAGE_SIZE, PAGE_SIZE), PAGE_SIZE)

    @pl.when((ph == 0) & (g == 0))
    def _():
        for h in range(N_HEADS):
            q16_scr[h] = _pad_rows(q_ref[0, :, head_cols(h)] * ATTN_SCALE, pad16).astype(BF16)
        if mode == "moba":
            ksum_scr[...] = jnp.zeros(ksum_scr.shape, F32)

    @pl.when(ph == 0)
    def _():
        for i in range(pp):
            page = g * pp + i
            for h in range(N_HEADS):
                kh = _head_slab(k_refs[i], h).astype(BF16)
                s = lax.dot_general(q16_scr[h], kh, NT_DIMS, preferred_element_type=F32)[0:t_new]
                if mode == "fox":
                    s = s - ck_ref[h:h + 1, i * PAGE_SIZE:(i + 1) * PAGE_SIZE]
                s_scr[head_rows(h), page_lanes(page)] = s
            if mode == "moba":
                for grp_i in range(HEAD_GROUPS):
                    heads = slice(grp_i * HEAD_GROUP, (grp_i + 1) * HEAD_GROUP)
                    ksum_scr[page // (MOBA_BLOCK // PAGE_SIZE), heads, :] += jnp.sum(k_refs[i][grp_i][...], axis=0)

    @pl.when((ph == 0) & (g == n_groups - 1))
    def _():
        cqs, gates = [], []
        for h in range(N_HEADS):
            kn = _pad_rows(kn_ref[0, :, head_cols(h)], LANES).astype(BF16)
            s_new = lax.dot_general(q16_scr[h], kn, NT_DIMS, preferred_element_type=F32)[0:t_new]
            if mode == "fox":
                c_new = cn_ref[h:h + 1, :]
                cqs.append(jnp.sum(jnp.where(lane == sub, jnp.broadcast_to(c_new, (t_new, LANES)), 0.0),
                                   axis=1, keepdims=True))
                s_new = s_new - c_new
            else:
                kmean = ksum_scr[:, h, :] / MOBA_BLOCK
                gates.append(lax.dot_general(q_ref[0, :, head_cols(h)], kmean, NT_DIMS, precision=HIGHEST,
                                             preferred_element_type=F32))
            s_scr[head_rows(h), past:past + LANES] = jnp.where(lane <= sub, s_new, -jnp.inf)

        if mode == "fox":
            cq = jnp.concatenate(cqs, axis=0)
        else:
            gate = jnp.concatenate(gates, axis=0)
            sel = _topk_mask(gate, gate == gate, MOBA_TOPK)

            def mask_block(b, _):
                sl = pl.ds(pl.multiple_of(b * MOBA_BLOCK, MOBA_BLOCK), MOBA_BLOCK)
                s_scr[:, sl] = jnp.where(_column(sel, b) > 0.5, s_scr[:, sl], -jnp.inf)
                return 0
            lax.fori_loop(0, past // MOBA_BLOCK, mask_block, 0)

        def run_max(c, m):
            sl = pl.ds(pl.multiple_of(c * LANES, LANES), LANES)
            s = s_scr[:, sl]
            if mode == "fox":
                s = s + cq
                s_scr[:, sl] = s
            return jnp.maximum(m, s)
        m = lax.fori_loop(0, n_chunks, run_max, jnp.full((n_rows, LANES), -jnp.inf, F32))
        m = jnp.max(m, axis=1, keepdims=True)

        def run_exp(c, l):
            sl = pl.ds(pl.multiple_of(c * LANES, LANES), LANES)
            p = jnp.exp(s_scr[:, sl] - m)
            s_scr[:, sl] = p
            return l + p
        l = lax.fori_loop(0, n_chunks, run_exp, jnp.zeros((n_rows, LANES), F32))
        l_scr[...] = jnp.sum(l, axis=1, keepdims=True)
        acc_scr[...] = jnp.zeros(acc_scr.shape, F32)

    def pv(h, lanes, v_slab):
        p = _pad_rows(s_scr[head_rows(h), lanes], pad16).astype(BF16)
        return jnp.dot(p, v_slab.astype(BF16), preferred_element_type=F32)[0:t_new]

    @pl.when(ph == 1)
    def _():
        for i in range(pp):
            page = g * pp + i
            for h in range(N_HEADS):
                acc_scr[head_rows(h), :] += pv(h, page_lanes(page), _head_slab(v_refs[i], h))

    @pl.when((ph == 1) & (g == n_groups - 1))
    def _():
        for h in range(N_HEADS):
            o = acc_scr[head_rows(h), :] + pv(h, slice(past, past + LANES),
                                              _pad_rows(vn_ref[0, :, head_cols(h)], LANES))
            o_ref[0, :, head_cols(h)] = (o / l_scr[head_rows(h), :]).astype(BF16)


def _sample_attn(q32, k32, v32, cum, cache_k, cache_v, page_table, slot, *, mode):
    bs, n_pages = page_table.shape
    t_new = q32.shape[0] // bs
    pp = PAGES_PER_STEP
    n_groups = n_pages // pp
    n_rows = N_HEADS * t_new
    past = n_pages * PAGE_SIZE

    def grp(ph, g, first):
        return jnp.where(ph == 0, g, n_groups - 1) if first else jnp.where(ph == 0, 0, g)

    def page_map(i, grp_i, first):
        return lambda b, ph, g, pt: (slot, pt[b, grp(ph, g, first) * pp + i], 0, grp_i, 0, 0)

    cache_k = cache_k.reshape(cache_k.shape[:3] + (HEAD_GROUPS, HEAD_GROUP, HEAD_DIM))
    cache_v = cache_v.reshape(cache_v.shape[:3] + (HEAD_GROUPS, HEAD_GROUP, HEAD_DIM))
    new_spec = pl.BlockSpec((1, t_new, D_MODEL), lambda b, ph, g, pt: (b, 0, 0))
    page_block = (None, None, PAGE_SIZE, None, HEAD_GROUP, HEAD_DIM)
    page_slots = [(i, grp_i) for i in range(pp) for grp_i in range(HEAD_GROUPS)]
    in_specs = [new_spec, new_spec, new_spec]
    args = [q32.reshape(bs, t_new, D_MODEL), k32.reshape(bs, t_new, D_MODEL), v32.reshape(bs, t_new, D_MODEL)]
    scratch = [pltpu.VMEM((n_rows, past + LANES), F32),
               pltpu.VMEM((n_rows, HEAD_DIM), F32),
               pltpu.VMEM((n_rows, 1), F32),
               pltpu.VMEM((N_HEADS, 2 * t_new, HEAD_DIM), BF16)]
    if mode == "fox":
        in_specs += [pl.BlockSpec((None, N_HEADS, pp * PAGE_SIZE), lambda b, ph, g, pt: (b, 0, grp(ph, g, True))),
                     pl.BlockSpec((None, N_HEADS, PAGE_SIZE), lambda b, ph, g, pt: (b, 0, 0))]
        args += list(cum)
    else:
        scratch.append(pltpu.VMEM((past // MOBA_BLOCK, N_HEADS, HEAD_DIM), F32))
    in_specs += [pl.BlockSpec(page_block, page_map(i, grp_i, True)) for i, grp_i in page_slots]
    args += [cache_k] * len(page_slots)
    in_specs += [pl.BlockSpec(page_block, page_map(i, grp_i, False)) for i, grp_i in page_slots]
    args += [cache_v] * len(page_slots)
    out = pl.pallas_call(
        functools.partial(_sample_attn_kernel, mode=mode, n_groups=n_groups, t_new=t_new),
        grid_spec=pltpu.PrefetchScalarGridSpec(
            num_scalar_prefetch=1, grid=(bs, 2, n_groups), in_specs=in_specs,
            out_specs=pl.BlockSpec((1, t_new, D_MODEL), lambda b, ph, g, pt: (b, 0, 0)),
            scratch_shapes=scratch),
        out_shape=jax.ShapeDtypeStruct((bs, t_new, D_MODEL), BF16),
        compiler_params=_params("arbitrary", "arbitrary", "arbitrary"), name=f"sample_attn_{mode}")(
            page_table, *args)
    return out.reshape(bs * t_new, D_MODEL)


def _oproj_kernel(o_ref, w_ref, x_ref, y_ref):
    y_ref[...] = x_ref[...] + jnp.dot(o_ref[...], w_ref[...], preferred_element_type=F32)


def _oproj(o16, w16, layer, x, *, tm, tn=512):
    m = x.shape[0]
    return pl.pallas_call(
        _oproj_kernel, grid=(m // tm, D_MODEL // tn),
        in_specs=[pl.BlockSpec((tm, D_MODEL), lambda i, n: (i, 0)),
                  pl.BlockSpec((None, D_MODEL, tn), lambda i, n: (layer, 0, n)),
                  pl.BlockSpec((tm, tn), lambda i, n: (i, n))],
        out_specs=pl.BlockSpec((tm, tn), lambda i, n: (i, n)),
        out_shape=jax.ShapeDtypeStruct((m, D_MODEL), F32),
        compiler_params=_params("arbitrary", "arbitrary"), name="oproj")(o16, w16, x)


def _conv_gate(gx_scr, g, cw_ref, cb_ref, g1_fix=None, g2_fix=None):
    tm = g.shape[0]
    gx_scr[HALO:HALO + tm, :] = g
    g1 = gx_scr[HALO - 1:HALO - 1 + tm, :]
    g2 = gx_scr[HALO - 2:HALO - 2 + tm, :]
    if g1_fix is not None:
        g1, g2 = g1_fix(g1), g2_fix(g2)
    acc = cb_ref[...] + g2 * cw_ref[0:1, :]
    acc = acc + g1 * cw_ref[1:2, :]
    return acc + g * cw_ref[2:3, :]


def _ffn_prompt_kernel(x_ref, xp_ref, gn_ref, wg_ref, wu_ref, cw_ref, cb_ref, wd_ref, y_ref, gt_ref,
                       h_scr, hp_scr, gx_scr, a_scr, *, tiles_per_seq):
    i, f = pl.program_id(0), pl.program_id(1)
    tm = x_ref.shape[0]
    last = pl.num_programs(1) - 1

    @pl.when(f == 0)
    def _():
        h_scr[...] = _rms_rows(x_ref[...], gn_ref[...]).astype(BF16)
        hp_scr[...] = _rms_rows(xp_ref[...], gn_ref[...]).astype(BF16)
        y_ref[...] = x_ref[...]

    def down_previous_tile():
        y_ref[...] += jnp.dot(a_scr[...], wd_ref[...], preferred_element_type=F32)

    def up_tile():
        g = jnp.dot(h_scr[...], wg_ref[...], preferred_element_type=F32)
        u = jnp.dot(h_scr[...], wu_ref[...], preferred_element_type=F32)
        g_prev = jnp.dot(hp_scr[...], wg_ref[...], preferred_element_type=F32)
        gx_scr[0:HALO, :] = jnp.where(i % tiles_per_seq == 0, 0.0, g_prev)
        a = _conv_gate(gx_scr, g, cw_ref, cb_ref)
        a_scr[...] = ((a * jax.nn.sigmoid(a)) * u).astype(BF16)
        gt_ref[0] = g[tm - 8:tm, :]

    @pl.when(f == 0)
    def _():
        up_tile()

    @pl.when((f > 0) & (f < last))
    def _():
        down_previous_tile()
        up_tile()

    @pl.when(f == last)
    def _():
        down_previous_tile()


def _ffn_prompt(x, gn, wg16, wu16, cw, cb, wd16, layer, *, seq, tm=1024, tf=256):
    m = x.shape[0]
    nf = D_FF // tf
    hb = tm // HALO

    def up(f):
        return jnp.minimum(f, nf - 1)

    y, gt = pl.pallas_call(
        functools.partial(_ffn_prompt_kernel, tiles_per_seq=seq // tm),
        grid=(m // tm, nf + 1),
        in_specs=[pl.BlockSpec((tm, D_MODEL), lambda i, f: (i, 0)),
                  pl.BlockSpec((HALO, D_MODEL), lambda i, f: (jnp.maximum(i * hb - 1, 0), 0)),
                  pl.BlockSpec((1, D_MODEL), lambda i, f: (0, 0)),
                  pl.BlockSpec((None, D_MODEL, tf), lambda i, f: (layer, 0, up(f))),
                  pl.BlockSpec((None, D_MODEL, tf), lambda i, f: (layer, 0, up(f))),
                  pl.BlockSpec((None, CONV_W, tf), lambda i, f: (layer, 0, up(f))),
                  pl.BlockSpec((None, 1, tf), lambda i, f: (layer, 0, up(f))),
                  pl.BlockSpec((None, tf, D_MODEL), lambda i, f: (layer, jnp.maximum(f - 1, 0), 0))],
        out_specs=[pl.BlockSpec((tm, D_MODEL), lambda i, f: (i, 0)),
                   pl.BlockSpec((1, 8, tf), lambda i, f: (i, 0, up(f)))],
        out_shape=[jax.ShapeDtypeStruct((m, D_MODEL), F32),
                   jax.ShapeDtypeStruct((m // tm, 8, D_FF), F32)],
        scratch_shapes=[pltpu.VMEM((tm, D_MODEL), BF16), pltpu.VMEM((HALO, D_MODEL), BF16),
                        pltpu.VMEM((tm + HALO, tf), F32), pltpu.VMEM((tm, tf), BF16)],
        compiler_params=_params("arbitrary", "arbitrary"), name="ffn_prompt")(
            x, x, gn.reshape(1, D_MODEL), wg16, wu16, cw, cb.reshape(cb.shape[0], 1, D_FF), wd16)
    tiles = seq // tm
    tail = gt.reshape(m // seq, tiles, 8, D_FF)[:, -1, 8 - (CONV_W - 1):, :]
    return y, tail


def _ffn_sample_kernel(x_ref, gn_ref, wg_ref, wu_ref, cw_ref, cb_ref, wd_ref, s1_ref, s2_ref, y_ref, g_ref,
                       h_scr, acc_scr, gx_scr, *, t_new):
    f = pl.program_id(0)

    @pl.when(f == 0)
    def _():
        h_scr[...] = _rms_rows(x_ref[...], gn_ref[...]).astype(BF16)
        acc_scr[...] = jnp.zeros(acc_scr.shape, F32)
        gx_scr[...] = jnp.zeros(gx_scr.shape, F32)

    g = jnp.dot(h_scr[...], wg_ref[...], preferred_element_type=F32)
    u = jnp.dot(h_scr[...], wu_ref[...], preferred_element_type=F32)
    tok = lax.broadcasted_iota(jnp.int32, g.shape, 0) % t_new
    a = _conv_gate(gx_scr, g, cw_ref, cb_ref,
                   g1_fix=lambda g1: jnp.where(tok >= 1, g1, s1_ref[...]),
                   g2_fix=lambda g2: jnp.where(tok >= 2, g2, s2_ref[...]))
    a = (a * jax.nn.sigmoid(a)) * u
    acc_scr[...] += jnp.dot(a.astype(BF16), wd_ref[...], preferred_element_type=F32)
    g_ref[...] = g

    @pl.when(f == pl.num_programs(0) - 1)
    def _():
        y_ref[...] = x_ref[...] + acc_scr[...]


def _ffn_sample(x, gn, wg16, wu16, cw, cb, wd16, layer, state, *, tf=512):
    m = x.shape[0]
    bs = state.shape[0]
    t_new = m // bs
    s1 = jnp.broadcast_to(state[:, -1:, :], (bs, t_new, D_FF)).reshape(m, D_FF)
    s2 = jnp.tile(state, (1, t_new // (CONV_W - 1), 1)).reshape(m, D_FF)
    y, g = pl.pallas_call(
        functools.partial(_ffn_sample_kernel, t_new=t_new),
        grid=(D_FF // tf,),
        in_specs=[pl.BlockSpec((m, D_MODEL), lambda f: (0, 0)),
                  pl.BlockSpec((1, D_MODEL), lambda f: (0, 0)),
                  pl.BlockSpec((None, D_MODEL, tf), lambda f: (layer, 0, f)),
                  pl.BlockSpec((None, D_MODEL, tf), lambda f: (layer, 0, f)),
                  pl.BlockSpec((None, CONV_W, tf), lambda f: (layer, 0, f)),
                  pl.BlockSpec((None, 1, tf), lambda f: (layer, 0, f)),
                  pl.BlockSpec((None, tf, D_MODEL), lambda f: (layer, f, 0)),
                  pl.BlockSpec((m, tf), lambda f: (0, f)),
                  pl.BlockSpec((m, tf), lambda f: (0, f))],
        out_specs=[pl.BlockSpec((m, D_MODEL), lambda f: (0, 0)),
                   pl.BlockSpec((m, tf), lambda f: (0, f))],
        out_shape=[jax.ShapeDtypeStruct((m, D_MODEL), F32), jax.ShapeDtypeStruct((m, D_FF), F32)],
        scratch_shapes=[pltpu.VMEM((m, D_MODEL), BF16), pltpu.VMEM((m, D_MODEL), F32),
                        pltpu.VMEM((m + HALO, tf), F32)],
        compiler_params=_params("arbitrary"), name="ffn_sample")(
            x, gn.reshape(1, D_MODEL), wg16, wu16, cw, cb.reshape(cb.shape[0], 1, D_FF), wd16, s1, s2)
    return y, g.reshape(bs, t_new, D_FF)[:, t_new - (CONV_W - 1):, :]


def _rope_tables(pos):
    half = HEAD_DIM // 2
    inv_freq = ROPE_THETA ** (-jnp.arange(half, dtype=F32) / half)
    ang = pos.astype(F32)[:, None] * inv_freq[None, :]
    cos, sin = jnp.cos(ang), jnp.sin(ang)
    return jnp.concatenate([cos, cos], axis=-1), jnp.concatenate([-sin, sin], axis=-1)


def kernel(x_prompt, x_sample, cache_fox_k, cache_fox_v, cache_fox_logf, cache_moba_k, cache_moba_v,
           state_ffn_conv, page_table, attn_norm, ffn_norm, fox_w_in, fox_b_f, fox_q_norm, fox_k_norm, fox_w_o,
           moba_w_in, moba_q_norm, moba_k_norm, moba_w_o, ffn_w_gate, ffn_w_up, ffn_conv_w, ffn_conv_b,
           ffn_w_down):
    B, T, _ = x_prompt.shape
    Bs, Ts, _ = x_sample.shape
    depth = attn_norm.shape[0]
    past_len = page_table.shape[1] * PAGE_SIZE
    mp, ms = B * T, Bs * Ts
    tm_p = 1024

    fox_w16 = fox_w_in.astype(BF16)
    fox_wf16 = fox_w16[:, :, 3 * D_MODEL:]
    moba_w16 = moba_w_in.astype(BF16)
    fox_wo16, moba_wo16 = fox_w_o.astype(BF16), moba_w_o.astype(BF16)
    wg16, wu16, wd16 = ffn_w_gate.astype(BF16), ffn_w_up.astype(BF16), ffn_w_down.astype(BF16)

    cos_p, sin_p = _rope_tables(jnp.arange(T, dtype=jnp.int32))
    cos_s, sin_s = _rope_tables(past_len + jnp.arange(Ts, dtype=jnp.int32))
    cos_s, sin_s = jnp.tile(cos_s, (Bs, 1)), jnp.tile(sin_s, (Bs, 1))

    xp, xs = x_prompt.reshape(mp, D_MODEL), x_sample.reshape(ms, D_MODEL)
    outs = {k: [] for k in ("fkp", "fvp", "flp", "fks", "fvs", "fls", "mkp", "mvp", "mks", "mvs", "cvp", "cvs")}
    heads_p = (B, T, N_HEADS, HEAD_DIM)
    heads_s = (Bs, Ts, N_HEADS, HEAD_DIM)
    for i in range(depth):
        j = i // 2
        if i % 2 == 0:
            w16, gq, gk = fox_w16, fox_q_norm[j], fox_k_norm[j]
            qp, kp, kp16, vp, vp16 = _proj(xp, attn_norm[i], w16, j, gq, gk, tm=tm_p)
            lp = _logf_proj(xp, attn_norm[i], fox_wf16[j], fox_b_f[j], tm=tm_p)
            qs, ks, _, vs, _ = _proj(xs, attn_norm[i], w16, j, gq, gk, tm=ms)
            ls = _logf_proj(xs, attn_norm[i], fox_wf16[j], fox_b_f[j], tm=ms)
            lp_rows = lp.reshape(B, T, N_HEADS).transpose(0, 2, 1).reshape(B * N_HEADS, T)
            c_rows = _cumsum_rows(lp_rows)
            op = _prompt_attn(qp, kp16, vp16, c_rows, mode="fox", batch=B, seq=T)
            cum_s = _past_cum(ls, cache_fox_logf, page_table, j)
            os_ = _sample_attn(qs, ks, vs, cum_s, cache_fox_k, cache_fox_v, page_table, j, mode="fox")
            wo16 = fox_wo16
            outs["fkp"].append(kp.reshape(heads_p)); outs["fvp"].append(vp.reshape(heads_p))
            outs["flp"].append(lp.reshape(B, T, N_HEADS))
            outs["fks"].append(ks.reshape(heads_s)); outs["fvs"].append(vs.reshape(heads_s))
            outs["fls"].append(ls.reshape(Bs, Ts, N_HEADS))
        else:
            w16, gq, gk = moba_w16, moba_q_norm[j], moba_k_norm[j]
            qp, kp, kp16, vp, vp16 = _proj(xp, attn_norm[i], w16, j, gq, gk, cos=cos_p, sin=sin_p, tm=tm_p)
            qs, ks, _, vs, _ = _proj(xs, attn_norm[i], w16, j, gq, gk, cos=cos_s, sin=sin_s, tm=ms)
            kmean = _block_mean(kp).reshape(B, T // MOBA_BLOCK, D_MODEL)
            op = _prompt_attn(qp, kp16, vp16, kmean, mode="moba", batch=B, seq=T)
            os_ = _sample_attn(qs, ks, vs, None, cache_moba_k, cache_moba_v, page_table, j, mode="moba")
            wo16 = moba_wo16
            outs["mkp"].append(kp.reshape(heads_p)); outs["mvp"].append(vp.reshape(heads_p))
            outs["mks"].append(ks.reshape(heads_s)); outs["mvs"].append(vs.reshape(heads_s))
        xp = _oproj(op, wo16, j, xp, tm=tm_p)
        xs = _oproj(os_, wo16, j, xs, tm=ms)
        xp, sp = _ffn_prompt(xp, ffn_norm[i], wg16, wu16, ffn_conv_w, ffn_conv_b, wd16, i, seq=T)
        xs, ss = _ffn_sample(xs, ffn_norm[i], wg16, wu16, ffn_conv_w, ffn_conv_b, wd16, i, state_ffn_conv[i])
        outs["cvp"].append(sp); outs["cvs"].append(ss)
    st = lambda k: jnp.stack(outs[k])
    return (xp.reshape(B, T, D_MODEL), xs.reshape(Bs, Ts, D_MODEL),
            st("fkp"), st("fvp"), st("flp"), st("mkp"), st("mvp"), st("cvp"),
            st("fks"), st("fvs"), st("fls"), st("mks"), st("mvs"), st("cvs"))
```
